```python
import math
import jax, jax.numpy as jnp
from jax import lax
import numpy as np


D_MODEL = 1024
BATCH = 8
SEQ = 2048
DEPTH = 4

N_MEM = 256
HEAD_DIM = 64
BLOCK = 128
WINDOW = 128
WIN_HEADS = 4
WIN_KV_HEADS = 2
DIFF_HEADS = 4
DIFF_QK_DIM = 64
DIFF_V_DIM = 2 * DIFF_QK_DIM
RWKV_HEADS = 4
RWKV_HEAD = 64
W_LORA = 64
A_LORA = 64
G_LORA = 128
X_HEADS = 4
X_HEAD_DIM = 128
D_FF = -(-8 * D_MODEL // (3 * 256)) * 256
RMS_EPS = 1e-6
GN_EPS = 64e-5

WIN_WIDTH = WIN_HEADS * HEAD_DIM
DIFF_WIDTH = DIFF_HEADS * DIFF_V_DIM
RWKV_WIDTH = RWKV_HEADS * RWKV_HEAD
MIX_WIDTH = WIN_WIDTH + DIFF_WIDTH + RWKV_WIDTH
RWKV_IN = 3 * RWKV_WIDTH + W_LORA + A_LORA + G_LORA
IN_SIZES = (WIN_WIDTH, WIN_KV_HEADS * HEAD_DIM, WIN_KV_HEADS * HEAD_DIM,
            2 * DIFF_HEADS * DIFF_QK_DIM, 2 * DIFF_HEADS * DIFF_QK_DIM, DIFF_WIDTH, RWKV_IN)
IN_WIDTH = sum(IN_SIZES)
RWKV_SIZES = (RWKV_WIDTH, RWKV_WIDTH, RWKV_WIDTH, W_LORA, A_LORA, G_LORA)

kernel_name = 'hybrid_parallel_heads_bidir_encoder'

f32 = jnp.float32


def split_last(z, sizes):
    idx = [int(i) for i in np.cumsum(sizes)[:-1]]
    return jnp.split(z, idx, axis=-1)


def rmsnorm(z, g):
    z32 = z.astype(f32)
    return (z32 * lax.rsqrt(jnp.mean(z32 * z32, axis=-1, keepdims=True) + RMS_EPS)).astype(z.dtype) * g


def alibi_slopes():
    n = WIN_HEADS + DIFF_HEADS
    s = jnp.exp2(-8.0 * jnp.arange(1, n + 1, dtype=f32) / n)
    return s[0::2], s[1::2]


def window_gqa(q, k, v, sink, slopes):
    b, t = q.shape[:2]
    nb = t // BLOCK
    g = WIN_HEADS // WIN_KV_HEADS
    qb = q.reshape(b, nb, BLOCK, WIN_KV_HEADS, g, HEAD_DIM)

    def band(z):
        zp = jnp.pad(z, ((0, 0), (BLOCK, BLOCK), (0, 0), (0, 0)))
        zb = zp.reshape(b, nb + 2, BLOCK, WIN_KV_HEADS, HEAD_DIM)
        return jnp.concatenate([zb[:, :-2], zb[:, 1:-1], zb[:, 2:]], axis=2)

    kb, vb = band(k), band(v)
    s = jnp.einsum('bnqhgd,bnkhd->bnhgqk', qb, kb).astype(f32) * (HEAD_DIM ** -0.5)
    qi = jnp.arange(BLOCK)[:, None]
    ki = jnp.arange(3 * BLOCK)[None, :] - BLOCK
    delta = ki - qi
    key_pos = jnp.arange(nb)[:, None] * BLOCK + ki
    in_range = (key_pos >= 0) & (key_pos < t)
    valid = (jnp.abs(delta) <= WINDOW)[None] & in_range[:, None, :]
    bias = -slopes.reshape(WIN_KV_HEADS, g, 1, 1) * jnp.abs(delta).astype(f32)
    s = jnp.where(valid[None, :, None, None], s + bias, -jnp.inf)
    sink_col = jnp.broadcast_to(sink.astype(f32).reshape(WIN_KV_HEADS, g, 1, 1), s.shape[:-1] + (1,))
    p = jax.nn.softmax(jnp.concatenate([s, sink_col], axis=-1), axis=-1)[..., :-1]
    o = jnp.einsum('bnhgqk,bnkhd->bnqhgd', p.astype(v.dtype), vb)
    return o.reshape(b, t, WIN_WIDTH)


def diff_attention(q, k, v, lam, lam_init, subln_g, slopes):
    b, t = q.shape[:2]
    nb = t // BLOCK
    qb = jnp.moveaxis(q.reshape(b, nb, BLOCK, DIFF_HEADS, 2, DIFF_QK_DIM), 1, 0)
    kpos = jnp.arange(t)

    def block(args):
        qblk, i = args
        s = jnp.einsum('bqhcd,bkhcd->bhcqk', qblk, k).astype(f32) * (DIFF_QK_DIM ** -0.5)
        qpos = i * BLOCK + jnp.arange(BLOCK)
        dist = jnp.abs(kpos[None, :] - qpos[:, None]).astype(f32)
        s = s - slopes[None, :, None, None, None] * dist[None, None, None]
        p = jax.nn.softmax(s, axis=-1)
        a = p[:, :, 0] - lam * p[:, :, 1]
        return jnp.einsum('bhqk,bkhe->bqhe', a.astype(v.dtype), v)

    o = lax.map(block, (qb, jnp.arange(nb)))
    o = jnp.moveaxis(o, 0, 1).reshape(b, t, DIFF_HEADS, DIFF_V_DIM)
    o = rmsnorm(o, subln_g) * (1.0 - lam_init)
    return o.reshape(b, t, DIFF_WIDTH)


def heads(z):
    return z.reshape(z.shape[:-1] + (RWKV_HEADS, RWKV_HEAD))


def time_major(z):
    z = jnp.stack([z[0], jnp.flip(z[1], axis=1)])
    return jnp.moveaxis(z, 2, 0).astype(f32)


def rwkv7_bidir(c, conv, w0, w_up, a0, a_up, g_up, k_k, k_a, r_k, ln_g, ln_b):
    b, t, _ = c.shape
    dt = c.dtype
    cp = jnp.pad(c, ((0, 0), (1, 1), (0, 0)))
    c = cp[:, :-2] * conv[0] + cp[:, 1:-1] * conv[1] + cp[:, 2:] * conv[2]
    r, k, v, wd, ad, gd = split_last(c.astype(f32), RWKV_SIZES)
    w_log = -jax.nn.softplus(-(w0[:, None, None] + jnp.einsum('btl,dlc->dbtc', jnp.tanh(wd), w_up))) - 0.5
    decay = jnp.exp(-jnp.exp(w_log))
    a = jax.nn.sigmoid(a0[:, None, None] + jnp.einsum('btl,dlc->dbtc', ad, a_up))
    g = jax.nn.sigmoid(gd) @ g_up
    kk = heads(k * k_k)
    kk = kk / jnp.maximum(jnp.linalg.norm(kk, axis=-1, keepdims=True), 1e-12)
    kd = heads(k * (1.0 + (a - 1.0) * k_a))
    rh, vh = heads(r), heads(v)
    shp = kd.shape
    xs = tuple(time_major(z) for z in (heads(decay), jnp.broadcast_to(kk, shp), kk * heads(a), kd,
                                       jnp.broadcast_to(vh, shp), jnp.broadcast_to(rh, shp)))

    def step(S, inp):
        wt, kkt, kat, kt, vt, rt = inp
        S = (S * wt[..., None, :]
             - jnp.einsum('...vk,...k->...v', S, kkt)[..., :, None] * kat[..., None, :]
             + vt[..., :, None] * kt[..., None, :])
        return S, jnp.einsum('...vk,...k->...v', S, rt)

    S0 = jnp.zeros((2, b, RWKV_HEADS, RWKV_HEAD, RWKV_HEAD), f32)
    _, y = lax.scan(step, S0, xs)
    y = jnp.moveaxis(y, 0, 2)
    y = y[0] + jnp.flip(y[1], axis=1)
    mu = jnp.mean(y, axis=-1, keepdims=True)
    var = jnp.mean(jnp.square(y - mu), axis=-1, keepdims=True)
    yn = (y - mu) * lax.rsqrt(var + GN_EPS) * heads(ln_g.astype(f32)) + heads(ln_b.astype(f32))
    bonus = jnp.sum(jnp.sum(rh * kd * r_k, axis=-1, keepdims=True), axis=0) * vh
    return ((yn + bonus).reshape(b, t, RWKV_WIDTH) * g).astype(dt)


def cross_attn(h, mem_n, wq, wkv, wo):
    b, t, _ = h.shape
    m = mem_n.shape[1]
    q = (h @ wq).reshape(b, t, X_HEADS, X_HEAD_DIM)
    kv = (mem_n @ wkv).reshape(b, m, 2, X_HEADS, X_HEAD_DIM)
    s = jnp.einsum('bqhd,bkhd->bhqk', q, kv[:, :, 0]).astype(f32) * (X_HEAD_DIM ** -0.5)
    p = jax.nn.softmax(s, axis=-1)
    o = jnp.einsum('bhqk,bkhd->bqhd', p.astype(h.dtype), kv[:, :, 1]).reshape(b, t, X_HEADS * X_HEAD_DIM)
    return o @ wo


def setup_inputs(seed: int = 0) -> dict:
    key = jax.random.key(seed)
    ks = iter(jax.random.split(key, 40))
    L = DEPTH

    def nrm(shape, scale):
        return jax.random.normal(next(ks), shape, f32) * scale

    def gain(shape):
        return 1.0 + nrm(shape, 0.02)

    return {
        'x': nrm((BATCH, SEQ, D_MODEL), 1.0),
        'mem': nrm((BATCH, N_MEM, D_MODEL), 1.0),
        'norm_mix': gain((L, D_MODEL)),
        'w_in': nrm((L, D_MODEL, IN_WIDTH), D_MODEL ** -0.5),
        'win_sink': nrm((L, WIN_HEADS), 0.5),
        'diff_lq1': nrm((L, DIFF_QK_DIM), 0.1),
        'diff_lk1': nrm((L, DIFF_QK_DIM), 0.1),
        'diff_lq2': nrm((L, DIFF_QK_DIM), 0.1),
        'diff_lk2': nrm((L, DIFF_QK_DIM), 0.1),
        'diff_subln': gain((L, DIFF_V_DIM)),
        'rwkv_conv': jnp.array([0.3, 1.0, 0.3], f32)[None, :, None] + nrm((L, 3, RWKV_IN), 0.1),
        'rwkv_w0': jax.random.uniform(next(ks), (L, 2, RWKV_WIDTH), f32, -6.0, 1.0),
        'rwkv_w_up': nrm((L, 2, W_LORA, RWKV_WIDTH), 0.1),
        'rwkv_a0': nrm((L, 2, RWKV_WIDTH), 0.5),
        'rwkv_a_up': nrm((L, 2, A_LORA, RWKV_WIDTH), A_LORA ** -0.5),
        'rwkv_g_up': nrm((L, G_LORA, RWKV_WIDTH), G_LORA ** -0.5),
        'rwkv_k_k': 0.85 + nrm((L, RWKV_WIDTH), 0.05),
        'rwkv_k_a': 1.0 + nrm((L, RWKV_WIDTH), 0.05),
        'rwkv_r_k': nrm((L, RWKV_HEADS, RWKV_HEAD), 0.1),
        'rwkv_ln_g': gain((L, RWKV_WIDTH)),
        'rwkv_ln_b': nrm((L, RWKV_WIDTH), 0.02),
        'w_out': nrm((L, MIX_WIDTH, D_MODEL), MIX_WIDTH ** -0.5),
        'norm_cross': gain((L, D_MODEL)),
        'norm_mem': gain((D_MODEL,)),
        'x_wq': nrm((L, D_MODEL, X_HEADS * X_HEAD_DIM), D_MODEL ** -0.5),
        'x_wkv': nrm((L, D_MODEL, 2 * X_HEADS * X_HEAD_DIM), D_MODEL ** -0.5),
        'x_wo': nrm((L, X_HEADS * X_HEAD_DIM, D_MODEL), (X_HEADS * X_HEAD_DIM) ** -0.5),
        'norm_ffn': gain((L, D_MODEL)),
        'ffn_w_gu': nrm((L, D_MODEL, 2 * D_FF), D_MODEL ** -0.5),
        'ffn_w_down': nrm((L, D_FF, D_MODEL), D_FF ** -0.5),
        'norm_final': gain((D_MODEL,)),
    }


def reference(x, mem, norm_mix, w_in, win_sink, diff_lq1, diff_lk1, diff_lq2, diff_lk2, diff_subln,
              rwkv_conv, rwkv_w0, rwkv_w_up, rwkv_a0, rwkv_a_up, rwkv_g_up, rwkv_k_k, rwkv_k_a, rwkv_r_k,
              rwkv_ln_g, rwkv_ln_b, w_out, norm_cross, norm_mem, x_wq, x_wkv, x_wo, norm_ffn,
              ffn_w_gu, ffn_w_down, norm_final):
    b, t, _ = x.shape
    slopes_win, slopes_diff = alibi_slopes()
    mem_n = rmsnorm(mem, norm_mem)
    for l in range(DEPTH):
        h = rmsnorm(x, norm_mix[l])
        aq, ak, av, bq, bk, bv, cin = split_last(h @ w_in[l], IN_SIZES)
        o_win = window_gqa(aq.reshape(b, t, WIN_HEADS, HEAD_DIM),
                           ak.reshape(b, t, WIN_KV_HEADS, HEAD_DIM),
                           av.reshape(b, t, WIN_KV_HEADS, HEAD_DIM), win_sink[l], slopes_win)
        lam_init = 0.8 - 0.6 * math.exp(-0.3 * l)
        lam = (jnp.exp(jnp.sum((diff_lq1[l] * diff_lk1[l]).astype(f32)))
               - jnp.exp(jnp.sum((diff_lq2[l] * diff_lk2[l]).astype(f32))) + lam_init)
        o_diff = diff_attention(bq.reshape(b, t, DIFF_HEADS, 2, DIFF_QK_DIM),
                                bk.reshape(b, t, DIFF_HEADS, 2, DIFF_QK_DIM),
                                bv.reshape(b, t, DIFF_HEADS, DIFF_V_DIM),
                                lam, lam_init, diff_subln[l], slopes_diff)
        o_rwkv = rwkv7_bidir(cin, rwkv_conv[l], rwkv_w0[l], rwkv_w_up[l], rwkv_a0[l], rwkv_a_up[l],
                             rwkv_g_up[l], rwkv_k_k[l], rwkv_k_a[l], rwkv_r_k[l], rwkv_ln_g[l], rwkv_ln_b[l])
        x = x + jnp.concatenate([o_win, o_diff, o_rwkv], axis=-1) @ w_out[l]
        x = x + cross_attn(rmsnorm(x, norm_cross[l]), mem_n, x_wq[l], x_wkv[l], x_wo[l])
        gate, up = jnp.split(rmsnorm(x, norm_ffn[l]) @ ffn_w_gu[l], 2, axis=-1)
        x = x + (jax.nn.silu(gate) * up) @ ffn_w_down[l]
    return rmsnorm(x, norm_final)
```

```python
import functools
import math

import jax
import jax.numpy as jnp
from jax import lax
from jax.experimental import pallas as pl
from jax.experimental.pallas import tpu as pltpu

f32 = jnp.float32
bf16 = jnp.bfloat16

D_MODEL = 1024
DEPTH = 4
HEAD_DIM = 64
BLOCK = 128
WINDOW = 128
WIN_HEADS = 4
WIN_KV_HEADS = 2
DIFF_HEADS = 4
DIFF_QK_DIM = 64
DIFF_V_DIM = 2 * DIFF_QK_DIM
RWKV_HEADS = 4
RWKV_HEAD = 64
W_LORA = 64
A_LORA = 64
G_LORA = 128
X_HEADS = 4
X_HEAD_DIM = 128
D_FF = -(-8 * D_MODEL // (3 * 256)) * 256
RMS_EPS = 1e-6
GN_EPS = 64e-5

WIN_WIDTH = WIN_HEADS * HEAD_DIM
DIFF_WIDTH = DIFF_HEADS * DIFF_V_DIM
RWKV_WIDTH = RWKV_HEADS * RWKV_HEAD
RWKV_IN = 3 * RWKV_WIDTH + W_LORA + A_LORA + G_LORA
IN_WIDTH = 2 * WIN_WIDTH + 3 * DIFF_WIDTH + RWKV_IN

OFF_AQ = 0
OFF_AK = WIN_WIDTH
OFF_AV = OFF_AK + WIN_KV_HEADS * HEAD_DIM
OFF_BQ = OFF_AV + WIN_KV_HEADS * HEAD_DIM
OFF_BK = OFF_BQ + DIFF_WIDTH
OFF_BV = OFF_BK + DIFF_WIDTH
OFF_C = OFF_BV + DIFF_WIDTH

CHUNK = 64
INV_BLOCK = 16
WIDE = RWKV_HEADS * CHUNK
V7X_VMEM_LIMIT = 56 * 1024 * 1024

_SLOPES = [2.0 ** (-8.0 * i / (WIN_HEADS + DIFF_HEADS)) for i in range(1, WIN_HEADS + DIFF_HEADS + 1)]
SLOPES_WIN = _SLOPES[0::2]
SLOPES_DIFF = _SLOPES[1::2]


def _params(*sem):
    return pltpu.CompilerParams(dimension_semantics=sem, vmem_limit_bytes=V7X_VMEM_LIMIT)


def _dot(a, b):
    return jnp.dot(a.astype(bf16), b.astype(bf16), preferred_element_type=f32)


def _dot_nt(a, b):
    return lax.dot_general(a.astype(bf16), b.astype(bf16), (((1,), (1,)), ((), ())),
                           preferred_element_type=f32)


def _split2(a):
    hi = a.astype(bf16)
    lo = (a - hi.astype(f32)).astype(bf16)
    return hi, lo


def _split3(a):
    hi = a.astype(bf16)
    r1 = a - hi.astype(f32)
    mid = r1.astype(bf16)
    lo = (r1 - mid.astype(f32)).astype(bf16)
    return hi, mid, lo


def _dot3(a, b, nt=False):
    d = _dot_nt if nt else _dot
    ah, al = _split2(a)
    bh, bl = _split2(b)
    return d(ah, bh) + (d(ah, bl) + d(al, bh))


def _dot_exact_rhs(a_bf16, b):
    b1, b2, b3 = _split3(b)
    return (jnp.dot(a_bf16, b1, preferred_element_type=f32)
            + (jnp.dot(a_bf16, b2, preferred_element_type=f32)
               + jnp.dot(a_bf16, b3, preferred_element_type=f32)))


def _dot_exact_lhs(a, b_bf16):
    a1, a2, a3 = _split3(a)
    return (jnp.dot(a1, b_bf16, preferred_element_type=f32)
            + (jnp.dot(a2, b_bf16, preferred_element_type=f32)
               + jnp.dot(a3, b_bf16, preferred_element_type=f32)))


def _rms(x, g):
    ms = jnp.mean(x * x, axis=-1, keepdims=True)
    return (x * lax.rsqrt(ms + RMS_EPS)) * g


def _sigmoid(z):
    return 1.0 / (1.0 + jnp.exp(-z))


def _norm_matmul_kernel(x_ref, g_ref, w_ref, o_ref, h_ref):
    @pl.when(pl.program_id(1) == 0)
    def _():
        h_ref[...] = _rms(x_ref[...], g_ref[...]).astype(bf16)

    o_ref[...] = jnp.dot(h_ref[...], w_ref[...], preferred_element_type=f32).astype(o_ref.dtype)


def norm_matmul(x, g, w, tm, tn, out_dtype=f32):
    m, k = x.shape
    n = w.shape[1]
    return pl.pallas_call(
        _norm_matmul_kernel,
        grid=(m // tm, n // tn),
        in_specs=[pl.BlockSpec((tm, k), lambda i, j: (i, 0)),
                  pl.BlockSpec((1, k), lambda i, j: (0, 0)),
                  pl.BlockSpec((k, tn), lambda i, j: (0, j))],
        out_specs=pl.BlockSpec((tm, tn), lambda i, j: (i, j)),
        out_shape=jax.ShapeDtypeStruct((m, n), out_dtype),
        scratch_shapes=[pltpu.VMEM((tm, k), bf16)],
        compiler_params=_params("parallel", "arbitrary"),
        name="norm_matmul",
    )(x, g.reshape(1, k), w)


def _win_kernel(sink_ref, q_ref, k_ref, v_ref, o_ref, *, seq):
    i = pl.program_id(1)
    span = 3 * BLOCK
    start = pl.multiple_of(jnp.clip((i - 1) * BLOCK, 0, seq - span), BLOCK)
    kw = k_ref[0, pl.ds(start, span), :]
    vw = v_ref[0, pl.ds(start, span), :]
    q = q_ref[0] * (HEAD_DIM ** -0.5)
    qpos = i * BLOCK + lax.broadcasted_iota(jnp.int32, (BLOCK, span), 0)
    kpos = start + lax.broadcasted_iota(jnp.int32, (BLOCK, span), 1)
    dist = jnp.abs(kpos - qpos)
    valid = dist <= WINDOW
    distf = dist.astype(f32)
    group = WIN_HEADS // WIN_KV_HEADS
    outs = []
    for head in range(WIN_HEADS):
        kvh = head // group
        kh = kw[:, kvh * HEAD_DIM:(kvh + 1) * HEAD_DIM]
        vh = vw[:, kvh * HEAD_DIM:(kvh + 1) * HEAD_DIM]
        qh = q[:, head * HEAD_DIM:(head + 1) * HEAD_DIM]
        s = _dot_nt(qh, kh)
        s = jnp.where(valid, s - SLOPES_WIN[head] * distf, -jnp.inf)
        sink = sink_ref[head]
        m = jnp.maximum(jnp.max(s, axis=-1, keepdims=True), sink)
        e = jnp.exp(s - m)
        denom = jnp.sum(e, axis=-1, keepdims=True) + jnp.exp(sink - m)
        outs.append(_dot(e, vh) / denom)
    o_ref[0] = jnp.concatenate(outs, axis=-1).astype(o_ref.dtype)


def window_attention(proj, sink):
    b, t, _ = proj.shape
    kvw = WIN_KV_HEADS * HEAD_DIM
    return pl.pallas_call(
        functools.partial(_win_kernel, seq=t),
        grid=(b, t // BLOCK),
        in_specs=[pl.BlockSpec(memory_space=pltpu.SMEM),
                  pl.BlockSpec((1, BLOCK, WIN_WIDTH), lambda bi, i: (bi, i, OFF_AQ // WIN_WIDTH)),
                  pl.BlockSpec((1, t, kvw), lambda bi, i: (bi, 0, OFF_AK // kvw)),
                  pl.BlockSpec((1, t, kvw), lambda bi, i: (bi, 0, OFF_AV // kvw))],
        out_specs=pl.BlockSpec((1, BLOCK, WIN_WIDTH), lambda bi, i: (bi, i, 0)),
        out_shape=jax.ShapeDtypeStruct((b, t, WIN_WIDTH), bf16),
        compiler_params=_params("parallel", "arbitrary"),
        name="window_attention",
    )(sink, proj, proj, proj)


def _diff_kernel(slope_ref, lq1_ref, lk1_ref, lq2_ref, lk2_ref, g_ref, q_ref, k_ref, v_ref, o_ref,
                 *, seq, tq, lam_init):
    h = pl.program_id(1)
    i = pl.program_id(2)
    lam = (jnp.exp(jnp.sum(lq1_ref[...] * lk1_ref[...], axis=-1, keepdims=True))
           - jnp.exp(jnp.sum(lq2_ref[...] * lk2_ref[...], axis=-1, keepdims=True)) + lam_init)
    q = q_ref[0] * (DIFF_QK_DIM ** -0.5)
    k = k_ref[0]
    v = v_ref[0]
    qpos = i * tq + lax.broadcasted_iota(jnp.int32, (tq, seq), 0)
    kpos = lax.broadcasted_iota(jnp.int32, (tq, seq), 1)
    bias = slope_ref[h] * jnp.abs(kpos - qpos).astype(f32)
    probs = []
    for c in range(2):
        s = _dot_nt(q[:, c * DIFF_QK_DIM:(c + 1) * DIFF_QK_DIM], k[:, c * DIFF_QK_DIM:(c + 1) * DIFF_QK_DIM]) - bias
        e = jnp.exp(s - jnp.max(s, axis=-1, keepdims=True))
        probs.append(e / jnp.sum(e, axis=-1, keepdims=True))
    a = probs[0] - lam * probs[1]
    o = _dot(a, v)
    o_ref[0] = (_rms(o, g_ref[...]) * (1.0 - lam_init)).astype(o_ref.dtype)


def diff_attention(proj, lq1, lk1, lq2, lk2, subln, lam_init, tq):
    b, t, _ = proj.shape
    w = DIFF_V_DIM
    vec = pl.BlockSpec((1, DIFF_QK_DIM), lambda bi, h, i: (0, 0))
    slopes = jnp.asarray(SLOPES_DIFF, f32)
    return pl.pallas_call(
        functools.partial(_diff_kernel, seq=t, tq=tq, lam_init=lam_init),
        grid=(b, DIFF_HEADS, t // tq),
        in_specs=[pl.BlockSpec(memory_space=pltpu.SMEM), vec, vec, vec, vec,
                  pl.BlockSpec((1, w), lambda bi, h, i: (0, 0)),
                  pl.BlockSpec((1, tq, w), lambda bi, h, i: (bi, i, OFF_BQ // w + h)),
                  pl.BlockSpec((1, t, w), lambda bi, h, i: (bi, 0, OFF_BK // w + h)),
                  pl.BlockSpec((1, t, w), lambda bi, h, i: (bi, 0, OFF_BV // w + h))],
        out_specs=pl.BlockSpec((1, tq, w), lambda bi, h, i: (bi, i, h)),
        out_shape=jax.ShapeDtypeStruct((b, t, DIFF_WIDTH), bf16),
        compiler_params=_params("parallel", "parallel", "arbitrary"),
        name="diff_attention",
    )(slopes, lq1.reshape(1, -1), lk1.reshape(1, -1), lq2.reshape(1, -1), lk2.reshape(1, -1),
      subln.reshape(1, -1), proj, proj, proj)


def _head_ones():
    r = lax.broadcasted_iota(jnp.int32, (RWKV_WIDTH, RWKV_WIDTH), 0) // RWKV_HEAD
    c = lax.broadcasted_iota(jnp.int32, (RWKV_WIDTH, RWKV_WIDTH), 1) // RWKV_HEAD
    return jnp.where(r == c, 1.0, 0.0).astype(bf16)


def _rwkv_prep_kernel(x_ref, xp_ref, xn_ref, conv_ref, w0_ref, wup_ref, a0_ref, aup_ref, gup_ref,
                      kk_ref, ka_ref, rk_ref,
                      r_out, k_out, v_out, kkn_out, g_out, bonus_out, lw_out, a_out, *, tt):
    i = pl.program_id(1)
    n = pl.num_programs(1)
    x = x_ref[0]
    first = jnp.where(i > 0, xp_ref[0, 7:8, :], 0.0)
    last = jnp.where(i < n - 1, xn_ref[0, 0:1, :], 0.0)
    row = lax.broadcasted_iota(jnp.int32, (tt, 1), 0)
    x_prev = jnp.where(row == 0, first, pltpu.roll(x, 1, 0))
    x_next = jnp.where(row == tt - 1, last, pltpu.roll(x, tt - 1, 0))
    c = x_prev * conv_ref[0:1, :] + x * conv_ref[1:2, :] + x_next * conv_ref[2:3, :]
    hw = RWKV_WIDTH
    r = c[:, 0:hw]
    k = c[:, hw:2 * hw]
    v = c[:, 2 * hw:3 * hw]
    o = 3 * hw
    wd = c[:, o:o + W_LORA]
    ad = c[:, o + W_LORA:o + W_LORA + A_LORA]
    gd = c[:, o + W_LORA + A_LORA:o + W_LORA + A_LORA + G_LORA]
    ones = _head_ones()
    kscaled = k * kk_ref[...]
    norm = jnp.sqrt(_dot_exact_lhs(kscaled * kscaled, ones))
    kkn = kscaled / jnp.maximum(norm, 1e-12)
    tw = jnp.tanh(wd)
    rk_sum = jnp.zeros_like(r)
    for d in range(2):
        z = w0_ref[d:d + 1, :] + _dot3(tw, wup_ref[d])
        lw_out[d, 0] = -math.exp(-0.5) * _sigmoid(z)
        a = _sigmoid(a0_ref[d:d + 1, :] + _dot3(ad, aup_ref[d]))
        a_out[d, 0] = a
        kd = k * (1.0 + (a - 1.0) * ka_ref[...])
        rk_sum = rk_sum + _dot_exact_lhs(r * kd * rk_ref[...], ones)
    r_out[0] = r
    k_out[0] = k
    v_out[0] = v
    kkn_out[0] = kkn
    g_out[0] = _dot(_sigmoid(gd), gup_ref[...])
    bonus_out[0] = rk_sum * v


def rwkv_prep(proj, conv, w0, w_up, a0, a_up, g_up, k_k, k_a, r_k, tt):
    b, t, _ = proj.shape
    hw = RWKV_WIDTH
    nblk8 = t // 8
    cb = OFF_C // RWKV_IN
    full = lambda shape: pl.BlockSpec(shape, lambda bi, i: (0,) * len(shape))
    tok = pl.BlockSpec((1, tt, hw), lambda bi, i: (bi, i, 0))
    tok2 = pl.BlockSpec((2, 1, tt, hw), lambda bi, i: (0, bi, i, 0))
    sds = jax.ShapeDtypeStruct((b, t, hw), f32)
    sds2 = jax.ShapeDtypeStruct((2, b, t, hw), f32)
    return pl.pallas_call(
        functools.partial(_rwkv_prep_kernel, tt=tt),
        grid=(b, t // tt),
        in_specs=[pl.BlockSpec((1, tt, RWKV_IN), lambda bi, i: (bi, i, cb)),
                  pl.BlockSpec((1, 8, RWKV_IN), lambda bi, i: (bi, jnp.maximum(i * (tt // 8) - 1, 0), cb)),
                  pl.BlockSpec((1, 8, RWKV_IN), lambda bi, i: (bi, jnp.minimum((i + 1) * (tt // 8), nblk8 - 1), cb)),
                  full((3, RWKV_IN)), full((2, hw)), full((2, W_LORA, hw)), full((2, hw)),
                  full((2, A_LORA, hw)), full((G_LORA, hw)), full((1, hw)), full((1, hw)), full((1, hw))],
        out_specs=[tok, tok, tok, tok, tok, tok, tok2, tok2],
        out_shape=[sds, sds, sds, sds, sds, sds, sds2, sds2],
        compiler_params=_params("parallel", "arbitrary"),
        name="rwkv_prep",
    )(proj, proj, proj, conv, w0, w_up, a0, a_up, g_up,
      k_k.reshape(1, hw), k_a.reshape(1, hw), r_k.reshape(1, hw))


def _stack_heads(x):
    lane_head = lax.broadcasted_iota(jnp.int32, x.shape, 1) // RWKV_HEAD
    return jnp.concatenate([jnp.where(lane_head == h, x, 0.0) for h in range(RWKV_HEADS)], axis=0)


def _unstack_heads(x):
    out = x[0:CHUNK]
    for h in range(1, RWKV_HEADS):
        out = out + x[h * CHUNK:(h + 1) * CHUNK]
    return out


def _unit_triangular_inverse(m, eye):
    r = lax.broadcasted_iota(jnp.int32, m.shape, 0) // INV_BLOCK
    c = lax.broadcasted_iota(jnp.int32, m.shape, 1) // INV_BLOCK
    d = jnp.where(r == c, m, 0.0)
    off = m - d
    x = -d
    x2 = _dot3(x, x)
    x4 = _dot3(x2, x2)
    x8 = _dot3(x4, x4)
    td = _dot3(_dot3(eye + x, eye + x2), _dot3(eye + x4, eye + x8))
    z = _dot3(td, off)
    z2 = _dot3(z, z)
    return _dot3(_dot3(eye - z, eye + z2), td)


def _rwkv_chunk(rev, lw, a, k, kkn, v, r, k_a, h_ref):
    ci = lax.broadcasted_iota(jnp.int32, (CHUNK, CHUNK), 0)
    cj = lax.broadcasted_iota(jnp.int32, (CHUNK, CHUNK), 1)
    tri = jnp.where((cj >= ci) if rev else (cj <= ci), 1.0, 0.0).astype(bf16)
    g_incl = _dot_exact_rhs(tri, lw)
    g_excl = g_incl - lw
    g_tot = g_incl[0:1, :] if rev else g_incl[CHUNK - 1:CHUNK, :]
    kat = kkn * a
    kd = k * (1.0 + (a - 1.0) * k_a)
    e_neg = jnp.exp(-g_incl)
    e_rem = jnp.exp(g_tot - g_incl)
    a_s = _stack_heads(kkn * jnp.exp(g_excl))
    r_s = _stack_heads(r * jnp.exp(g_incl))
    b_s = _stack_heads(kat * e_neg)
    k_s = _stack_heads(kd * e_neg)
    v_s = _stack_heads(v)
    b_hat = kat * e_rem
    k_hat = kd * e_rem

    wi = lax.broadcasted_iota(jnp.int32, (WIDE, WIDE), 0)
    wj = lax.broadcasted_iota(jnp.int32, (WIDE, WIDE), 1)
    ti = wi % CHUNK
    tj = wj % CHUNK
    strict = (tj > ti) if rev else (tj < ti)
    incl = (tj >= ti) if rev else (tj <= ti)
    same_head = (wi // RWKV_HEAD) == (wj // RWKV_HEAD)
    eye = jnp.where(wi == wj, 1.0, 0.0)

    m_ab = jnp.where(strict, _dot3(a_s, b_s, nt=True), 0.0)
    m_ak = jnp.where(strict, _dot3(a_s, k_s, nt=True), 0.0)
    a_rb = jnp.where(incl, _dot3(r_s, b_s, nt=True), 0.0)
    a_rk = jnp.where(incl, _dot3(r_s, k_s, nt=True), 0.0)
    t_inv = _unit_triangular_inverse(m_ab, eye)

    w_s = _dot(t_inv, a_s)
    u0_s = -_dot(t_inv, _dot(m_ak, v_s))
    rq = _unstack_heads(r_s - _dot(a_rb, w_s))
    y0 = _unstack_heads(_dot(a_rb, u0_s) + _dot(a_rk, v_s))
    w_f = _unstack_heads(w_s)
    u0_f = _unstack_heads(u0_s)
    b_hat_t = b_hat.T
    p = jnp.where(same_head, jnp.where(wi == wj, jnp.exp(g_tot), 0.0) - _dot(b_hat_t, w_f), 0.0)
    q = jnp.where(same_head, _dot(b_hat_t, u0_f) + _dot(k_hat.T, v), 0.0)
    h = h_ref[...]
    y = _dot(rq, h) + y0
    h_ref[...] = _dot(p, h) + q
    return y


def _rwkv_scan_kernel(ka_ref,
                      lw0, a0, k0, kk0, v0, r0,
                      lw1, a1, k1, kk1, v1, r1,
                      y0_ref, y1_ref, h0_ref, h1_ref):
    @pl.when(pl.program_id(1) == 0)
    def _():
        h0_ref[...] = jnp.zeros_like(h0_ref)
        h1_ref[...] = jnp.zeros_like(h1_ref)

    k_a = ka_ref[...]
    y0_ref[0] = _rwkv_chunk(False, lw0[0, 0], a0[0, 0], k0[0], kk0[0], v0[0], r0[0], k_a, h0_ref)
    y1_ref[0] = _rwkv_chunk(True, lw1[0, 0], a1[0, 0], k1[0], kk1[0], v1[0], r1[0], k_a, h1_ref)


def rwkv_scan(lw, a, k, kkn, v, r, k_a):
    b, t, hw = k.shape
    nc = t // CHUNK
    fwd = pl.BlockSpec((1, CHUNK, hw), lambda bi, c: (bi, c, 0))
    bwd = pl.BlockSpec((1, CHUNK, hw), lambda bi, c: (bi, nc - 1 - c, 0))
    fwd2 = pl.BlockSpec((1, 1, CHUNK, hw), lambda bi, c: (0, bi, c, 0))
    bwd2 = pl.BlockSpec((1, 1, CHUNK, hw), lambda bi, c: (1, bi, nc - 1 - c, 0))
    sds = jax.ShapeDtypeStruct((b, t, hw), f32)
    return pl.pallas_call(
        _rwkv_scan_kernel,
        grid=(b, nc),
        in_specs=[pl.BlockSpec((1, hw), lambda bi, c: (0, 0)),
                  fwd2, fwd2, fwd, fwd, fwd, fwd,
                  bwd2, bwd2, bwd, bwd, bwd, bwd],
        out_specs=[fwd, bwd],
        out_shape=[sds, sds],
        scratch_shapes=[pltpu.VMEM((WIDE, WIDE), f32), pltpu.VMEM((WIDE, WIDE), f32)],
        compiler_params=_params("parallel", "arbitrary"),
        name="rwkv_scan",
    )(k_a.reshape(1, hw), lw, a, k, kkn, v, r, lw, a, k, kkn, v, r)


def _rwkv_post_kernel(y0_ref, y1_ref, g_ref, bonus_ref, lng_ref, lnb_ref, o_ref):
    ones = _head_ones()
    y = y0_ref[...] + y1_ref[...]
    mu = _dot_exact_lhs(y, ones) * (1.0 / RWKV_HEAD)
    yc = y - mu
    var = _dot_exact_lhs(yc * yc, ones) * (1.0 / RWKV_HEAD)
    yn = yc * lax.rsqrt(var + GN_EPS) * lng_ref[...] + lnb_ref[...]
    o_ref[...] = ((yn + bonus_ref[...]) * g_ref[...]).astype(o_ref.dtype)


def rwkv_post(y0, y1, g, bonus, ln_g, ln_b, tm):
    m, hw = y0.shape
    tok = pl.BlockSpec((tm, hw), lambda i: (i, 0))
    vec = pl.BlockSpec((1, hw), lambda i: (0, 0))
    return pl.pallas_call(
        _rwkv_post_kernel,
        grid=(m // tm,),
        in_specs=[tok, tok, tok, tok, vec, vec],
        out_specs=tok,
        out_shape=jax.ShapeDtypeStruct((m, hw), bf16),
        compiler_params=_params("parallel"),
        name="rwkv_post",
    )(y0, y1, g, bonus, ln_g.reshape(1, hw), ln_b.reshape(1, hw))


def _out_proj_kernel(x_ref, a_ref, b_ref, c_ref, wa_ref, wb_ref, wc_ref, o_ref):
    acc = jnp.dot(a_ref[...], wa_ref[...], preferred_element_type=f32)
    acc = acc + jnp.dot(b_ref[...], wb_ref[...], preferred_element_type=f32)
    acc = acc + jnp.dot(c_ref[...], wc_ref[...], preferred_element_type=f32)
    o_ref[...] = x_ref[...] + acc


def out_proj(x, o_win, o_diff, o_rwkv, w_out, tm):
    m, d = x.shape
    wa = w_out[:WIN_WIDTH]
    wb = w_out[WIN_WIDTH:WIN_WIDTH + DIFF_WIDTH]
    wc = w_out[WIN_WIDTH + DIFF_WIDTH:]
    row = lambda w: pl.BlockSpec((tm, w), lambda i: (i, 0))
    full = lambda w: pl.BlockSpec((w, d), lambda i: (0, 0))
    return pl.pallas_call(
        _out_proj_kernel,
        grid=(m // tm,),
        in_specs=[row(d), row(WIN_WIDTH), row(DIFF_WIDTH), row(RWKV_WIDTH),
                  full(WIN_WIDTH), full(DIFF_WIDTH), full(RWKV_WIDTH)],
        out_specs=row(d),
        out_shape=jax.ShapeDtypeStruct((m, d), f32),
        compiler_params=_params("parallel"),
        name="out_proj",
    )(x, o_win, o_diff, o_rwkv, wa, wb, wc)


def _cross_kernel(x_ref, g_ref, wq_ref, kv_ref, wo_ref, o_ref):
    x = x_ref[0]
    q = jnp.dot(_rms(x, g_ref[...]).astype(bf16), wq_ref[...], preferred_element_type=f32)
    kv = kv_ref[0]
    xw = X_HEADS * X_HEAD_DIM
    outs = []
    for h in range(X_HEADS):
        qh = q[:, h * X_HEAD_DIM:(h + 1) * X_HEAD_DIM]
        kh = kv[:, h * X_HEAD_DIM:(h + 1) * X_HEAD_DIM]
        vh = kv[:, xw + h * X_HEAD_DIM:xw + (h + 1) * X_HEAD_DIM]
        s = _dot_nt(qh, kh) * (X_HEAD_DIM ** -0.5)
        e = jnp.exp(s - jnp.max(s, axis=-1, keepdims=True))
        p = e / jnp.sum(e, axis=-1, keepdims=True)
        outs.append(_dot(p, vh))
    o = jnp.concatenate(outs, axis=-1)
    o_ref[0] = x + jnp.dot(o.astype(bf16), wo_ref[...], preferred_element_type=f32)


def cross_attention(x, kv, g, wq, wo, tm):
    b, t, d = x.shape
    nm = kv.shape[1]
    xw = X_HEADS * X_HEAD_DIM
    return pl.pallas_call(
        _cross_kernel,
        grid=(b, t // tm),
        in_specs=[pl.BlockSpec((1, tm, d), lambda bi, i: (bi, i, 0)),
                  pl.BlockSpec((1, d), lambda bi, i: (0, 0)),
                  pl.BlockSpec((d, xw), lambda bi, i: (0, 0)),
                  pl.BlockSpec((1, nm, 2 * xw), lambda bi, i: (bi, 0, 0)),
                  pl.BlockSpec((xw, d), lambda bi, i: (0, 0))],
        out_specs=pl.BlockSpec((1, tm, d), lambda bi, i: (bi, i, 0)),
        out_shape=jax.ShapeDtypeStruct((b, t, d), f32),
        compiler_params=_params("parallel", "parallel"),
        name="cross_attention",
    )(x, g.reshape(1, d), wq, kv, wo)


def _ffn_up_kernel(x_ref, g_ref, wg_ref, wu_ref, o_ref, h_ref):
    @pl.when(pl.program_id(1) == 0)
    def _():
        h_ref[...] = _rms(x_ref[...], g_ref[...]).astype(bf16)

    h = h_ref[...]
    gate = jnp.dot(h, wg_ref[...], preferred_element_type=f32)
    up = jnp.dot(h, wu_ref[...], preferred_element_type=f32)
    o_ref[...] = (gate * _sigmoid(gate) * up).astype(o_ref.dtype)


def ffn_up(x, g, w_gu, tm, tf):
    m, d = x.shape
    nf = D_FF // tf
    return pl.pallas_call(
        _ffn_up_kernel,
        grid=(m // tm, nf),
        in_specs=[pl.BlockSpec((tm, d), lambda i, j: (i, 0)),
                  pl.BlockSpec((1, d), lambda i, j: (0, 0)),
                  pl.BlockSpec((d, tf), lambda i, j: (0, j)),
                  pl.BlockSpec((d, tf), lambda i, j: (0, j + nf))],
        out_specs=pl.BlockSpec((tm, tf), lambda i, j: (i, j)),
        out_shape=jax.ShapeDtypeStruct((m, D_FF), bf16),
        scratch_shapes=[pltpu.VMEM((tm, d), bf16)],
        compiler_params=_params("parallel", "arbitrary"),
        name="ffn_up",
    )(x, g.reshape(1, d), w_gu, w_gu)


def _ffn_down_kernel(x_ref, a_ref, w_ref, o_ref):
    o_ref[...] = x_ref[...] + jnp.dot(a_ref[...], w_ref[...], preferred_element_type=f32)


def ffn_down(x, act, w_down, tm):
    m, d = x.shape
    kf = act.shape[1]
    return pl.pallas_call(
        _ffn_down_kernel,
        grid=(m // tm,),
        in_specs=[pl.BlockSpec((tm, d), lambda i: (i, 0)),
                  pl.BlockSpec((tm, kf), lambda i: (i, 0)),
                  pl.BlockSpec((kf, d), lambda i: (0, 0))],
        out_specs=pl.BlockSpec((tm, d), lambda i: (i, 0)),
        out_shape=jax.ShapeDtypeStruct((m, d), f32),
        compiler_params=_params("parallel"),
        name="ffn_down",
    )(x, act, w_down)


def _final_norm_kernel(x_ref, g_ref, o_ref):
    o_ref[...] = _rms(x_ref[...], g_ref[...])


def final_norm(x, g, tm):
    m, d = x.shape
    return pl.pallas_call(
        _final_norm_kernel,
        grid=(m // tm,),
        in_specs=[pl.BlockSpec((tm, d), lambda i: (i, 0)), pl.BlockSpec((1, d), lambda i: (0, 0))],
        out_specs=pl.BlockSpec((tm, d), lambda i: (i, 0)),
        out_shape=jax.ShapeDtypeStruct((m, d), f32),
        compiler_params=_params("parallel"),
        name="final_norm",
    )(x, g.reshape(1, d))


def _row_tile(m, want):
    t = min(want, m)
    while m % t:
        t //= 2
    return t


def kernel(x, mem, norm_mix, w_in, win_sink, diff_lq1, diff_lk1, diff_lq2, diff_lk2, diff_subln,
           rwkv_conv, rwkv_w0, rwkv_w_up, rwkv_a0, rwkv_a_up, rwkv_g_up, rwkv_k_k, rwkv_k_a, rwkv_r_k,
           rwkv_ln_g, rwkv_ln_b, w_out, norm_cross, norm_mem, x_wq, x_wkv, x_wo, norm_ffn,
           ffn_w_gu, ffn_w_down, norm_final):
    b, t, d = x.shape
    nm = mem.shape[1]
    m = b * t
    tm = _row_tile(m, 512)
    depth = w_in.shape[0]
    w_in_b, w_out_b, wq_b, wkv_b, wo_b, wgu_b, wdn_b = (
        z.astype(bf16) for z in (w_in, w_out, x_wq, x_wkv, x_wo, ffn_w_gu, ffn_w_down))
    mem2 = mem.reshape(b * nm, d)
    xf = x.reshape(m, d)
    for l in range(depth):
        proj = norm_matmul(xf, norm_mix[l], w_in_b[l], tm, 1024).reshape(b, t, IN_WIDTH)
        o_win = window_attention(proj, win_sink[l])
        lam_init = 0.8 - 0.6 * math.exp(-0.3 * l)
        o_diff = diff_attention(proj, diff_lq1[l], diff_lk1[l], diff_lq2[l], diff_lk2[l], diff_subln[l],
                                lam_init, _row_tile(t, 256))
        r, k, v, kkn, g, bonus, lw, a = rwkv_prep(
            proj, rwkv_conv[l], rwkv_w0[l], rwkv_w_up[l], rwkv_a0[l], rwkv_a_up[l], rwkv_g_up[l],
            rwkv_k_k[l], rwkv_k_a[l], rwkv_r_k[l], _row_tile(t, 256))
        y0, y1 = rwkv_scan(lw, a, k, kkn, v, r, rwkv_k_a[l])
        o_rwkv = rwkv_post(y0.reshape(m, RWKV_WIDTH), y1.reshape(m, RWKV_WIDTH), g.reshape(m, RWKV_WIDTH),
                           bonus.reshape(m, RWKV_WIDTH), rwkv_ln_g[l], rwkv_ln_b[l], tm)
        xf = out_proj(xf, o_win.reshape(m, WIN_WIDTH), o_diff.reshape(m, DIFF_WIDTH), o_rwkv, w_out_b[l], tm)
        kv = norm_matmul(mem2, norm_mem, wkv_b[l], _row_tile(b * nm, 512), 1024).reshape(b, nm, -1)
        xf = cross_attention(xf.reshape(b, t, d), kv, norm_cross[l], wq_b[l], wo_b[l],
                             _row_tile(t, 256)).reshape(m, d)
        act = ffn_up(xf, norm_ffn[l], wgu_b[l], tm, D_FF // 2)
        xf = ffn_down(xf, act, wdn_b[l], tm)
    return final_norm(xf, norm_final, tm).reshape(b, t, d)
```

```python
import functools
import math

import jax
import jax.numpy as jnp
from jax import lax
from jax.experimental import pallas as pl
from jax.experimental.pallas import tpu as pltpu

f32 = jnp.float32
bf16 = jnp.bfloat16

D_MODEL = 1024
DEPTH = 4
HEAD_DIM = 64
BLOCK = 128
WINDOW = 128
WIN_HEADS = 4
WIN_KV_HEADS = 2
DIFF_HEADS = 4
DIFF_QK_DIM = 64
DIFF_V_DIM = 2 * DIFF_QK_DIM
RWKV_HEADS = 4
RWKV_HEAD = 64
W_LORA = 64
A_LORA = 64
G_LORA = 128
X_HEADS = 4
X_HEAD_DIM = 128
D_FF = -(-8 * D_MODEL // (3 * 256)) * 256
RMS_EPS = 1e-6
GN_EPS = 64e-5

WIN_WIDTH = WIN_HEADS * HEAD_DIM
DIFF_WIDTH = DIFF_HEADS * DIFF_V_DIM
RWKV_WIDTH = RWKV_HEADS * RWKV_HEAD
RWKV_IN = 3 * RWKV_WIDTH + W_LORA + A_LORA + G_LORA
IN_WIDTH = 2 * WIN_WIDTH + 3 * DIFF_WIDTH + RWKV_IN

OFF_AQ = 0
OFF_AK = WIN_WIDTH
OFF_AV = OFF_AK + WIN_KV_HEADS * HEAD_DIM
OFF_BQ = OFF_AV + WIN_KV_HEADS * HEAD_DIM
OFF_BK = OFF_BQ + DIFF_WIDTH
OFF_BV = OFF_BK + DIFF_WIDTH
OFF_C = OFF_BV + DIFF_WIDTH

CHUNK = 64
INV_BLOCK = 8
assert INV_BLOCK == 8 and CHUNK == 8 * INV_BLOCK
WIDE = RWKV_HEADS * CHUNK
V7X_VMEM_LIMIT = 56 * 1024 * 1024
LOG2E = 1.4426950408889634

_SLOPES = [2.0 ** (-8.0 * i / (WIN_HEADS + DIFF_HEADS)) for i in range(1, WIN_HEADS + DIFF_HEADS + 1)]
SLOPES_WIN = _SLOPES[0::2]
SLOPES_DIFF = _SLOPES[1::2]


def _params(*sem):
    return pltpu.CompilerParams(dimension_semantics=sem, vmem_limit_bytes=V7X_VMEM_LIMIT)


def _dot(a, b):
    return jnp.dot(a.astype(bf16), b.astype(bf16), preferred_element_type=f32)


def _dot_nt(a, b):
    return lax.dot_general(a.astype(bf16), b.astype(bf16), (((1,), (1,)), ((), ())),
                           preferred_element_type=f32)


def _split2(a):
    hi = a.astype(bf16)
    lo = (a - hi.astype(f32)).astype(bf16)
    return hi, lo


def _split3(a):
    hi = a.astype(bf16)
    r1 = a - hi.astype(f32)
    mid = r1.astype(bf16)
    lo = (r1 - mid.astype(f32)).astype(bf16)
    return hi, mid, lo


def _dot3(a, b, nt=False):
    d = _dot_nt if nt else _dot
    ah, al = _split2(a)
    bh, bl = _split2(b)
    return d(ah, bh) + (d(ah, bl) + d(al, bh))


def _dot_exact_rhs(a_bf16, b):
    b1, b2, b3 = _split3(b)
    return (jnp.dot(a_bf16, b1, preferred_element_type=f32)
            + (jnp.dot(a_bf16, b2, preferred_element_type=f32)
               + jnp.dot(a_bf16, b3, preferred_element_type=f32)))


def _dot_exact_lhs(a, b_bf16):
    a1, a2, a3 = _split3(a)
    return (jnp.dot(a1, b_bf16, preferred_element_type=f32)
            + (jnp.dot(a2, b_bf16, preferred_element_type=f32)
               + jnp.dot(a3, b_bf16, preferred_element_type=f32)))


def _rms(x, g):
    ms = jnp.mean(x * x, axis=-1, keepdims=True)
    return (x * lax.rsqrt(ms + RMS_EPS)) * g


def _sigmoid(z):
    return 1.0 / (1.0 + jnp.exp(-z))


def _norm_matmul_kernel(x_ref, g_ref, w_ref, o_ref, h_ref):
    @pl.when(pl.program_id(1) == 0)
    def _():
        h_ref[...] = _rms(x_ref[...], g_ref[...]).astype(bf16)

    o_ref[...] = jnp.dot(h_ref[...], w_ref[...], preferred_element_type=f32).astype(o_ref.dtype)


def norm_matmul(x, g, w, tm, tn, out_dtype=f32):
    m, k = x.shape
    n = w.shape[1]
    return pl.pallas_call(
        _norm_matmul_kernel,
        grid=(m // tm, n // tn),
        in_specs=[pl.BlockSpec((tm, k), lambda i, j: (i, 0)),
                  pl.BlockSpec((1, k), lambda i, j: (0, 0)),
                  pl.BlockSpec((k, tn), lambda i, j: (0, j))],
        out_specs=pl.BlockSpec((tm, tn), lambda i, j: (i, j)),
        out_shape=jax.ShapeDtypeStruct((m, n), out_dtype),
        scratch_shapes=[pltpu.VMEM((tm, k), bf16)],
        compiler_params=_params("parallel", "arbitrary"),
        name="norm_matmul",
    )(x, g.reshape(1, k), w)


def _win_kernel(sink_ref, q_ref, k_ref, v_ref, o_ref, *, seq):
    i = pl.program_id(1)
    span = 3 * BLOCK
    start = pl.multiple_of(jnp.clip((i - 1) * BLOCK, 0, seq - span), BLOCK)
    kw = k_ref[0, pl.ds(start, span), :]
    vw = v_ref[0, pl.ds(start, span), :]
    q = q_ref[0] * (HEAD_DIM ** -0.5)
    qpos = i * BLOCK + lax.broadcasted_iota(jnp.int32, (BLOCK, span), 0)
    kpos = start + lax.broadcasted_iota(jnp.int32, (BLOCK, span), 1)
    dist = jnp.abs(kpos - qpos)
    valid = dist <= WINDOW
    distf = dist.astype(f32)
    group = WIN_HEADS // WIN_KV_HEADS
    outs = []
    for head in range(WIN_HEADS):
        kvh = head // group
        kh = kw[:, kvh * HEAD_DIM:(kvh + 1) * HEAD_DIM]
        vh = vw[:, kvh * HEAD_DIM:(kvh + 1) * HEAD_DIM]
        qh = q[:, head * HEAD_DIM:(head + 1) * HEAD_DIM]
        s = _dot_nt(qh, kh)
        s = jnp.where(valid, s - SLOPES_WIN[head] * distf, -jnp.inf)
        sink = sink_ref[head]
        m = jnp.maximum(jnp.max(s, axis=-1, keepdims=True), sink)
        e = jnp.exp(s - m)
        denom = jnp.sum(e, axis=-1, keepdims=True) + jnp.exp(sink - m)
        outs.append(_dot(e, vh) / denom)
    o_ref[0] = jnp.concatenate(outs, axis=-1).astype(o_ref.dtype)


def window_attention(proj, sink):
    b, t, _ = proj.shape
    kvw = WIN_KV_HEADS * HEAD_DIM
    return pl.pallas_call(
        functools.partial(_win_kernel, seq=t),
        grid=(b, t // BLOCK),
        in_specs=[pl.BlockSpec(memory_space=pltpu.SMEM),
                  pl.BlockSpec((1, BLOCK, WIN_WIDTH), lambda bi, i: (bi, i, OFF_AQ // WIN_WIDTH)),
                  pl.BlockSpec((1, t, kvw), lambda bi, i: (bi, 0, OFF_AK // kvw)),
                  pl.BlockSpec((1, t, kvw), lambda bi, i: (bi, 0, OFF_AV // kvw))],
        out_specs=pl.BlockSpec((1, BLOCK, WIN_WIDTH), lambda bi, i: (bi, i, 0)),
        out_shape=jax.ShapeDtypeStruct((b, t, WIN_WIDTH), bf16),
        compiler_params=_params("parallel", "arbitrary"),
        name="window_attention",
    )(sink, proj, proj, proj)


def _diff_kernel(slope_ref, lq1_ref, lk1_ref, lq2_ref, lk2_ref, g_ref, dist_ref, q_ref, k_ref, v_ref, o_ref,
                 *, seq, tq, lam_init):
    h = pl.program_id(1)
    i = pl.program_id(2)
    lam = (jnp.exp(jnp.sum(lq1_ref[...] * lk1_ref[...], axis=-1, keepdims=True))
           - jnp.exp(jnp.sum(lq2_ref[...] * lk2_ref[...], axis=-1, keepdims=True)) + lam_init)
    q = q_ref[0] * (DIFF_QK_DIM ** -0.5 * LOG2E)
    k = k_ref[0]
    v = v_ref[0].astype(bf16)
    nkb = seq // BLOCK
    first = (seq // tq - 1 - i) * (tq // BLOCK)
    slope = slope_ref[h] * LOG2E
    bias = jnp.concatenate([slope * dist_ref[first + kb] for kb in range(nkb)], axis=1)
    outs = []
    for c in range(2):
        s = _dot_nt(q[:, c * DIFF_QK_DIM:(c + 1) * DIFF_QK_DIM], k[:, c * DIFF_QK_DIM:(c + 1) * DIFF_QK_DIM]) - bias
        e = jnp.exp2(s - jnp.max(s, axis=-1, keepdims=True))
        outs.append(jnp.dot(e.astype(bf16), v, preferred_element_type=f32) / jnp.sum(e, axis=-1, keepdims=True))
    o = outs[0] - lam * outs[1]
    o_ref[0] = (_rms(o, g_ref[...]) * (1.0 - lam_init)).astype(o_ref.dtype)


def _distance_table(seq, tq):
    off = (seq // tq - 1) * (tq // BLOCK)
    nj = seq // BLOCK + off
    j = lax.broadcasted_iota(jnp.int32, (nj, tq, BLOCK), 0)
    q = lax.broadcasted_iota(jnp.int32, (nj, tq, BLOCK), 1)
    kl = lax.broadcasted_iota(jnp.int32, (nj, tq, BLOCK), 2)
    return jnp.abs((j - off) * BLOCK + kl - q).astype(f32)


def diff_attention(proj, lq1, lk1, lq2, lk2, subln, lam_init, tq):
    b, t, _ = proj.shape
    w = DIFF_V_DIM
    vec = pl.BlockSpec((1, DIFF_QK_DIM), lambda bi, h, i: (0, 0))
    slopes = jnp.asarray(SLOPES_DIFF, f32)
    dist = _distance_table(t, tq)
    return pl.pallas_call(
        functools.partial(_diff_kernel, seq=t, tq=tq, lam_init=lam_init),
        grid=(b, DIFF_HEADS, t // tq),
        in_specs=[pl.BlockSpec(memory_space=pltpu.SMEM), vec, vec, vec, vec,
                  pl.BlockSpec((1, w), lambda bi, h, i: (0, 0)),
                  pl.BlockSpec(dist.shape, lambda bi, h, i: (0, 0, 0)),
                  pl.BlockSpec((1, tq, w), lambda bi, h, i: (bi, i, OFF_BQ // w + h)),
                  pl.BlockSpec((1, t, w), lambda bi, h, i: (bi, 0, OFF_BK // w + h)),
                  pl.BlockSpec((1, t, w), lambda bi, h, i: (bi, 0, OFF_BV // w + h))],
        out_specs=pl.BlockSpec((1, tq, w), lambda bi, h, i: (bi, i, h)),
        out_shape=jax.ShapeDtypeStruct((b, t, DIFF_WIDTH), bf16),
        compiler_params=_params("parallel", "parallel", "arbitrary"),
        name="diff_attention",
    )(slopes, lq1.reshape(1, -1), lk1.reshape(1, -1), lq2.reshape(1, -1), lk2.reshape(1, -1),
      subln.reshape(1, -1), dist, proj, proj, proj)


def _head_ones():
    r = lax.broadcasted_iota(jnp.int32, (RWKV_WIDTH, RWKV_WIDTH), 0) // RWKV_HEAD
    c = lax.broadcasted_iota(jnp.int32, (RWKV_WIDTH, RWKV_WIDTH), 1) // RWKV_HEAD
    return jnp.where(r == c, 1.0, 0.0).astype(bf16)


def _rwkv_prep_kernel(x_ref, xp_ref, xn_ref, conv_ref, w0_ref, wup_ref, a0_ref, aup_ref, gup_ref,
                      kk_ref, ka_ref, rk_ref,
                      r_out, k_out, v_out, kkn_out, g_out, bonus_out, lw_out, a_out, *, tt):
    i = pl.program_id(1)
    n = pl.num_programs(1)
    x = x_ref[0]
    first = jnp.where(i > 0, xp_ref[0, 7:8, :], 0.0)
    last = jnp.where(i < n - 1, xn_ref[0, 0:1, :], 0.0)
    row = lax.broadcasted_iota(jnp.int32, (tt, 1), 0)
    x_prev = jnp.where(row == 0, first, pltpu.roll(x, 1, 0))
    x_next = jnp.where(row == tt - 1, last, pltpu.roll(x, tt - 1, 0))
    c = x_prev * conv_ref[0:1, :] + x * conv_ref[1:2, :] + x_next * conv_ref[2:3, :]
    hw = RWKV_WIDTH
    r = c[:, 0:hw]
    k = c[:, hw:2 * hw]
    v = c[:, 2 * hw:3 * hw]
    o = 3 * hw
    wd = c[:, o:o + W_LORA]
    ad = c[:, o + W_LORA:o + W_LORA + A_LORA]
    gd = c[:, o + W_LORA + A_LORA:o + W_LORA + A_LORA + G_LORA]
    ones = _head_ones()
    kscaled = k * kk_ref[...]
    norm = jnp.sqrt(_dot_exact_lhs(kscaled * kscaled, ones))
    kkn = kscaled / jnp.maximum(norm, 1e-12)
    tw = jnp.tanh(wd)
    rk_sum = jnp.zeros_like(r)
    for d in range(2):
        z = w0_ref[d:d + 1, :] + _dot3(tw, wup_ref[d])
        lw_out[d, 0] = -math.exp(-0.5) * _sigmoid(z)
        a = _sigmoid(a0_ref[d:d + 1, :] + _dot3(ad, aup_ref[d]))
        a_out[d, 0] = a
        kd = k * (1.0 + (a - 1.0) * ka_ref[...])
        rk_sum = rk_sum + _dot_exact_lhs(r * kd * rk_ref[...], ones)
    r_out[0] = r
    k_out[0] = k
    v_out[0] = v
    kkn_out[0] = kkn
    g_out[0] = _dot(_sigmoid(gd), gup_ref[...])
    bonus_out[0] = rk_sum * v


def rwkv_prep(proj, conv, w0, w_up, a0, a_up, g_up, k_k, k_a, r_k, tt):
    b, t, _ = proj.shape
    hw = RWKV_WIDTH
    nblk8 = t // 8
    cb = OFF_C // RWKV_IN
    full = lambda shape: pl.BlockSpec(shape, lambda bi, i: (0,) * len(shape))
    tok = pl.BlockSpec((1, tt, hw), lambda bi, i: (bi, i, 0))
    tok2 = pl.BlockSpec((2, 1, tt, hw), lambda bi, i: (0, bi, i, 0))
    sds = jax.ShapeDtypeStruct((b, t, hw), f32)
    sds2 = jax.ShapeDtypeStruct((2, b, t, hw), f32)
    return pl.pallas_call(
        functools.partial(_rwkv_prep_kernel, tt=tt),
        grid=(b, t // tt),
        in_specs=[pl.BlockSpec((1, tt, RWKV_IN), lambda bi, i: (bi, i, cb)),
                  pl.BlockSpec((1, 8, RWKV_IN), lambda bi, i: (bi, jnp.maximum(i * (tt // 8) - 1, 0), cb)),
                  pl.BlockSpec((1, 8, RWKV_IN), lambda bi, i: (bi, jnp.minimum((i + 1) * (tt // 8), nblk8 - 1), cb)),
                  full((3, RWKV_IN)), full((2, hw)), full((2, W_LORA, hw)), full((2, hw)),
                  full((2, A_LORA, hw)), full((G_LORA, hw)), full((1, hw)), full((1, hw)), full((1, hw))],
        out_specs=[tok, tok, tok, tok, tok, tok, tok2, tok2],
        out_shape=[sds, sds, sds, sds, sds, sds, sds2, sds2],
        compiler_params=_params("parallel", "arbitrary"),
        name="rwkv_prep",
    )(proj, proj, proj, conv, w0, w_up, a0, a_up, g_up,
      k_k.reshape(1, hw), k_a.reshape(1, hw), r_k.reshape(1, hw))


def _head_masks():
    lane_head = lax.broadcasted_iota(jnp.int32, (CHUNK, WIDE), 1) // RWKV_HEAD
    return [jnp.where(lane_head == h, 1.0, 0.0).astype(bf16) for h in range(RWKV_HEADS)]


def _stack(x, masks):
    xb = x.astype(bf16)
    return jnp.concatenate([xb * mk for mk in masks], axis=0)


def _mm(a, b, masks):
    return _dot(a, _stack(b, masks))


def _each(fn, *lists):
    return [fn(*args) for args in zip(*lists)]


def _unit_triangular_inverse(ms, eye, masks):
    r = lax.broadcasted_iota(jnp.int32, eye.shape, 0) // INV_BLOCK
    c = (lax.broadcasted_iota(jnp.int32, eye.shape, 1) % CHUNK) // INV_BLOCK
    mm = lambda a, b: _mm(a, b, masks)
    d = [jnp.where(r == c, m, 0.0) for m in ms]
    off = _each(lambda m, dd: m - dd, ms, d)
    x = [-dd for dd in d]
    x2 = _each(lambda xx: mm(xx, xx), x)
    x4 = _each(lambda xx: mm(xx, xx), x2)
    p1 = _each(lambda u, w: mm(eye + u, eye + w), x, x2)
    td = _each(lambda u, w: mm(u, eye + w), p1, x4)
    z = _each(mm, td, off)
    z2 = _each(lambda zz: mm(zz, zz), z)
    z4 = _each(lambda zz: mm(zz, zz), z2)
    p3 = _each(lambda u, w: mm(eye - u, eye + w), z, z2)
    p4 = _each(lambda w, t: mm(eye + w, t), z4, td)
    return _each(mm, p3, p4)


def _rwkv_chunks(revs, lws, a_s, ks, kkns, vs, rs, k_a, h_refs):
    masks = _head_masks()
    ci = lax.broadcasted_iota(jnp.int32, (CHUNK, CHUNK), 0)
    cj = lax.broadcasted_iota(jnp.int32, (CHUNK, CHUNK), 1)
    tri_f = jnp.where(cj <= ci, 1.0, 0.0).astype(bf16)
    tri_r = jnp.where(cj >= ci, 1.0, 0.0).astype(bf16)
    ti = lax.broadcasted_iota(jnp.int32, (CHUNK, WIDE), 0)
    tj = lax.broadcasted_iota(jnp.int32, (CHUNK, WIDE), 1) % CHUNK
    eye = jnp.where(ti == tj, 1.0, 0.0)
    strict = [(tj > ti) if rev else (tj < ti) for rev in revs]
    incl = [(tj >= ti) if rev else (tj <= ti) for rev in revs]

    g_incl = [_dot_exact_rhs(tri_r if rev else tri_f, lw) for rev, lw in zip(revs, lws)]
    g_tot = [g[0:1, :] if rev else g[CHUNK - 1:CHUNK, :] for rev, g in zip(revs, g_incl)]
    kat = _each(lambda kkn, a: kkn * a, kkns, a_s)
    kd = _each(lambda k, a: k * (1.0 + (a - 1.0) * k_a), ks, a_s)
    e_neg = [jnp.exp(-g) for g in g_incl]
    e_rem = _each(lambda gt, g: jnp.exp(gt - g), g_tot, g_incl)
    a_t = _each(lambda kkn, g, lw: kkn * jnp.exp(g - lw), kkns, g_incl, lws)
    r_t = _each(lambda r, g: r * jnp.exp(g), rs, g_incl)
    b_t = _each(lambda x, e: x * e, kat, e_neg)
    k_t = _each(lambda x, e: x * e, kd, e_neg)
    b_hat = _each(lambda x, e: x * e, kat, e_rem)
    k_hat = _each(lambda x, e: x * e, kd, e_rem)

    ar = _each(lambda x, y: jnp.concatenate([x, y], axis=0).astype(bf16), a_t, r_t)
    s_b = _each(lambda x, y: _dot_nt(x, _stack(y, masks)), ar, b_t)
    s_k = _each(lambda x, y: _dot_nt(x, _stack(y, masks)), ar, k_t)
    m_ab = _each(lambda s, x: jnp.where(s, x[:CHUNK], 0.0), strict, s_b)
    a_rb = _each(lambda s, x: jnp.where(s, x[CHUNK:], 0.0), incl, s_b)
    m_ak = _each(lambda s, x: jnp.where(s, x[:CHUNK], 0.0), strict, s_k)
    a_rk = _each(lambda s, x: jnp.where(s, x[CHUNK:], 0.0), incl, s_k)
    t_inv = _unit_triangular_inverse(m_ab, eye, masks)

    wi = lax.broadcasted_iota(jnp.int32, (WIDE, WIDE), 0)
    wj = lax.broadcasted_iota(jnp.int32, (WIDE, WIDE), 1)
    same_head = (wi // RWKV_HEAD) == (wj // RWKV_HEAD)
    diag = wi == wj
    h = [h_ref[...].astype(bf16) for h_ref in h_refs]
    decay = [jnp.where(diag, jnp.exp(gt), 0.0).astype(bf16) for gt in g_tot]
    arh = _each(lambda x, dd, hh: _dot(jnp.concatenate([x, dd], axis=0), hh), ar, decay, h)
    v_s = _each(lambda x: _stack(x, masks), vs)
    mkv = _each(lambda m, n, vv: _dot(jnp.concatenate([m, n], axis=0), vv), m_ak, a_rk, v_s)
    u = _each(lambda t, x, y: -_mm(t, x[:CHUNK] + y[:CHUNK], masks), t_inv, arh, mkv)
    ys = _each(lambda x, m, uu, y: x[CHUNK:2 * CHUNK] + _mm(m, uu, masks) + y[CHUNK:], arh, a_rb, u, mkv)
    bk_t = _each(lambda x, y: jnp.concatenate([x, y], axis=0).T, b_hat, k_hat)
    for h_ref, x, bk, uu, vv in zip(h_refs, arh, bk_t, u, vs):
        upd = _dot(bk, jnp.concatenate([uu, vv], axis=0))
        h_ref[...] = x[2 * CHUNK:] + jnp.where(same_head, upd, 0.0)
    return ys


def _rwkv_scan_kernel(ka_ref,
                      lw0, a0, k0, kk0, v0, r0,
                      lw1, a1, k1, kk1, v1, r1,
                      y0_ref, y1_ref, h_ref, *, rows):
    @pl.when(pl.program_id(1) == 0)
    def _():
        h_ref[...] = jnp.zeros_like(h_ref)

    revs = [False] * rows + [True] * rows
    gather = lambda f, b: [f[0, g] for g in range(rows)] + [b[0, g] for g in range(rows)]
    plain = lambda f, b: [f[g] for g in range(rows)] + [b[g] for g in range(rows)]
    ys = _rwkv_chunks(revs, gather(lw0, lw1), gather(a0, a1), plain(k0, k1), plain(kk0, kk1),
                      plain(v0, v1), plain(r0, r1), ka_ref[...], [h_ref.at[i] for i in range(2 * rows)])
    for g in range(rows):
        y0_ref[g] = ys[g]
        y1_ref[g] = ys[rows + g]


def rwkv_scan(lw, a, k, kkn, v, r, k_a, rows):
    b, t, hw = k.shape
    nc = t // CHUNK
    fwd = pl.BlockSpec((rows, CHUNK, hw), lambda bi, c: (bi, c, 0))
    bwd = pl.BlockSpec((rows, CHUNK, hw), lambda bi, c: (bi, nc - 1 - c, 0))
    fwd2 = pl.BlockSpec((1, rows, CHUNK, hw), lambda bi, c: (0, bi, c, 0))
    bwd2 = pl.BlockSpec((1, rows, CHUNK, hw), lambda bi, c: (1, bi, nc - 1 - c, 0))
    sds = jax.ShapeDtypeStruct((b, t, hw), f32)
    return pl.pallas_call(
        functools.partial(_rwkv_scan_kernel, rows=rows),
        grid=(b // rows, nc),
        in_specs=[pl.BlockSpec((1, hw), lambda bi, c: (0, 0)),
                  fwd2, fwd2, fwd, fwd, fwd, fwd,
                  bwd2, bwd2, bwd, bwd, bwd, bwd],
        out_specs=[fwd, bwd],
        out_shape=[sds, sds],
        scratch_shapes=[pltpu.VMEM((2 * rows, WIDE, WIDE), f32)],
        compiler_params=_params("parallel", "arbitrary"),
        name="rwkv_scan",
    )(k_a.reshape(1, hw), lw, a, k, kkn, v, r, lw, a, k, kkn, v, r)


def _rwkv_post_kernel(y0_ref, y1_ref, g_ref, bonus_ref, lng_ref, lnb_ref, o_ref):
    ones = _head_ones()
    y = y0_ref[...] + y1_ref[...]
    mu = _dot_exact_lhs(y, ones) * (1.0 / RWKV_HEAD)
    yc = y - mu
    var = _dot_exact_lhs(yc * yc, ones) * (1.0 / RWKV_HEAD)
    yn = yc * lax.rsqrt(var + GN_EPS) * lng_ref[...] + lnb_ref[...]
    o_ref[...] = ((yn + bonus_ref[...]) * g_ref[...]).astype(o_ref.dtype)


def rwkv_post(y0, y1, g, bonus, ln_g, ln_b, tm):
    m, hw = y0.shape
    tok = pl.BlockSpec((tm, hw), lambda i: (i, 0))
    vec = pl.BlockSpec((1, hw), lambda i: (0, 0))
    return pl.pallas_call(
        _rwkv_post_kernel,
        grid=(m // tm,),
        in_specs=[tok, tok, tok, tok, vec, vec],
        out_specs=tok,
        out_shape=jax.ShapeDtypeStruct((m, hw), bf16),
        compiler_params=_params("parallel"),
        name="rwkv_post",
    )(y0, y1, g, bonus, ln_g.reshape(1, hw), ln_b.reshape(1, hw))


def _out_proj_kernel(x_ref, a_ref, b_ref, c_ref, wa_ref, wb_ref, wc_ref, o_ref):
    acc = jnp.dot(a_ref[...], wa_ref[...], preferred_element_type=f32)
    acc = acc + jnp.dot(b_ref[...], wb_ref[...], preferred_element_type=f32)
    acc = acc + jnp.dot(c_ref[...], wc_ref[...], preferred_element_type=f32)
    o_ref[...] = x_ref[...] + acc


def out_proj(x, o_win, o_diff, o_rwkv, w_out, tm):
    m, d = x.shape
    wa = w_out[:WIN_WIDTH]
    wb = w_out[WIN_WIDTH:WIN_WIDTH + DIFF_WIDTH]
    wc = w_out[WIN_WIDTH + DIFF_WIDTH:]
    row = lambda w: pl.BlockSpec((tm, w), lambda i: (i, 0))
    full = lambda w: pl.BlockSpec((w, d), lambda i: (0, 0))
    return pl.pallas_call(
        _out_proj_kernel,
        grid=(m // tm,),
        in_specs=[row(d), row(WIN_WIDTH), row(DIFF_WIDTH), row(RWKV_WIDTH),
                  full(WIN_WIDTH), full(DIFF_WIDTH), full(RWKV_WIDTH)],
        out_specs=row(d),
        out_shape=jax.ShapeDtypeStruct((m, d), f32),
        compiler_params=_params("parallel"),
        name="out_proj",
    )(x, o_win, o_diff, o_rwkv, wa, wb, wc)


def _cross_kernel(x_ref, g_ref, wq_ref, kv_ref, wo_ref, o_ref):
    x = x_ref[0]
    q = jnp.dot(_rms(x, g_ref[...]).astype(bf16), wq_ref[...], preferred_element_type=f32)
    kv = kv_ref[0]
    xw = X_HEADS * X_HEAD_DIM
    outs = []
    for h in range(X_HEADS):
        qh = q[:, h * X_HEAD_DIM:(h + 1) * X_HEAD_DIM]
        kh = kv[:, h * X_HEAD_DIM:(h + 1) * X_HEAD_DIM]
        vh = kv[:, xw + h * X_HEAD_DIM:xw + (h + 1) * X_HEAD_DIM]
        s = _dot_nt(qh, kh) * (X_HEAD_DIM ** -0.5)
        e = jnp.exp(s - jnp.max(s, axis=-1, keepdims=True))
        p = e / jnp.sum(e, axis=-1, keepdims=True)
        outs.append(_dot(p, vh))
    o = jnp.concatenate(outs, axis=-1)
    o_ref[0] = x + jnp.dot(o.astype(bf16), wo_ref[...], preferred_element_type=f32)


def cross_attention(x, kv, g, wq, wo, tm):
    b, t, d = x.shape
    nm = kv.shape[1]
    xw = X_HEADS * X_HEAD_DIM
    return pl.pallas_call(
        _cross_kernel,
        grid=(b, t // tm),
        in_specs=[pl.BlockSpec((1, tm, d), lambda bi, i: (bi, i, 0)),
                  pl.BlockSpec((1, d), lambda bi, i: (0, 0)),
                  pl.BlockSpec((d, xw), lambda bi, i: (0, 0)),
                  pl.BlockSpec((1, nm, 2 * xw), lambda bi, i: (bi, 0, 0)),
                  pl.BlockSpec((xw, d), lambda bi, i: (0, 0))],
        out_specs=pl.BlockSpec((1, tm, d), lambda bi, i: (bi, i, 0)),
        out_shape=jax.ShapeDtypeStruct((b, t, d), f32),
        compiler_params=_params("parallel", "parallel"),
        name="cross_attention",
    )(x, g.reshape(1, d), wq, kv, wo)


def _ffn_up_kernel(x_ref, g_ref, wg_ref, wu_ref, o_ref, h_ref):
    @pl.when(pl.program_id(1) == 0)
    def _():
        h_ref[...] = _rms(x_ref[...], g_ref[...]).astype(bf16)

    h = h_ref[...]
    gate = jnp.dot(h, wg_ref[...], preferred_element_type=f32)
    up = jnp.dot(h, wu_ref[...], preferred_element_type=f32)
    o_ref[...] = (gate * _sigmoid(gate) * up).astype(o_ref.dtype)


def ffn_up(x, g, w_gu, tm, tf):
    m, d = x.shape
    nf = D_FF // tf
    return pl.pallas_call(
        _ffn_up_kernel,
        grid=(m // tm, nf),
        in_specs=[pl.BlockSpec((tm, d), lambda i, j: (i, 0)),
                  pl.BlockSpec((1, d), lambda i, j: (0, 0)),
                  pl.BlockSpec((d, tf), lambda i, j: (0, j)),
                  pl.BlockSpec((d, tf), lambda i, j: (0, j + nf))],
        out_specs=pl.BlockSpec((tm, tf), lambda i, j: (i, j)),
        out_shape=jax.ShapeDtypeStruct((m, D_FF), bf16),
        scratch_shapes=[pltpu.VMEM((tm, d), bf16)],
        compiler_params=_params("parallel", "arbitrary"),
        name="ffn_up",
    )(x, g.reshape(1, d), w_gu, w_gu)


def _ffn_down_kernel(x_ref, a_ref, w_ref, o_ref):
    o_ref[...] = x_ref[...] + jnp.dot(a_ref[...], w_ref[...], preferred_element_type=f32)


def ffn_down(x, act, w_down, tm):
    m, d = x.shape
    kf = act.shape[1]
    return pl.pallas_call(
        _ffn_down_kernel,
        grid=(m // tm,),
        in_specs=[pl.BlockSpec((tm, d), lambda i: (i, 0)),
                  pl.BlockSpec((tm, kf), lambda i: (i, 0)),
                  pl.BlockSpec((kf, d), lambda i: (0, 0))],
        out_specs=pl.BlockSpec((tm, d), lambda i: (i, 0)),
        out_shape=jax.ShapeDtypeStruct((m, d), f32),
        compiler_params=_params("parallel"),
        name="ffn_down",
    )(x, act, w_down)


def _final_norm_kernel(x_ref, g_ref, o_ref):
    o_ref[...] = _rms(x_ref[...], g_ref[...])


def final_norm(x, g, tm):
    m, d = x.shape
    return pl.pallas_call(
        _final_norm_kernel,
        grid=(m // tm,),
        in_specs=[pl.BlockSpec((tm, d), lambda i: (i, 0)), pl.BlockSpec((1, d), lambda i: (0, 0))],
        out_specs=pl.BlockSpec((tm, d), lambda i: (i, 0)),
        out_shape=jax.ShapeDtypeStruct((m, d), f32),
        compiler_params=_params("parallel"),
        name="final_norm",
    )(x, g.reshape(1, d))


def _row_tile(m, want):
    t = min(want, m)
    while m % t:
        t //= 2
    return t


def kernel(x, mem, norm_mix, w_in, win_sink, diff_lq1, diff_lk1, diff_lq2, diff_lk2, diff_subln,
           rwkv_conv, rwkv_w0, rwkv_w_up, rwkv_a0, rwkv_a_up, rwkv_g_up, rwkv_k_k, rwkv_k_a, rwkv_r_k,
           rwkv_ln_g, rwkv_ln_b, w_out, norm_cross, norm_mem, x_wq, x_wkv, x_wo, norm_ffn,
           ffn_w_gu, ffn_w_down, norm_final):
    b, t, d = x.shape
    nm = mem.shape[1]
    m = b * t
    tm = _row_tile(m, 512)
    depth = w_in.shape[0]
    w_in_b, w_out_b, wq_b, wkv_b, wo_b, wgu_b, wdn_b = (
        z.astype(bf16) for z in (w_in, w_out, x_wq, x_wkv, x_wo, ffn_w_gu, ffn_w_down))
    mem2 = mem.reshape(b * nm, d)
    xf = x.reshape(m, d)
    for l in range(depth):
        proj = norm_matmul(xf, norm_mix[l], w_in_b[l], tm, 1024).reshape(b, t, IN_WIDTH)
        o_win = window_attention(proj, win_sink[l])
        lam_init = 0.8 - 0.6 * math.exp(-0.3 * l)
        o_diff = diff_attention(proj, diff_lq1[l], diff_lk1[l], diff_lq2[l], diff_lk2[l], diff_subln[l],
                                lam_init, _row_tile(t, 256))
        r, k, v, kkn, g, bonus, lw, a = rwkv_prep(
            proj, rwkv_conv[l], rwkv_w0[l], rwkv_w_up[l], rwkv_a0[l], rwkv_a_up[l], rwkv_g_up[l],
            rwkv_k_k[l], rwkv_k_a[l], rwkv_r_k[l], _row_tile(t, 256))
        y0, y1 = rwkv_scan(lw, a, k, kkn, v, r, rwkv_k_a[l], 4 if b % 4 == 0 else 1)
        o_rwkv = rwkv_post(y0.reshape(m, RWKV_WIDTH), y1.reshape(m, RWKV_WIDTH), g.reshape(m, RWKV_WIDTH),
                           bonus.reshape(m, RWKV_WIDTH), rwkv_ln_g[l], rwkv_ln_b[l], tm)
        xf = out_proj(xf, o_win.reshape(m, WIN_WIDTH), o_diff.reshape(m, DIFF_WIDTH), o_rwkv, w_out_b[l], tm)
        kv = norm_matmul(mem2, norm_mem, wkv_b[l], _row_tile(b * nm, 512), 1024).reshape(b, nm, -1)
        xf = cross_attention(xf.reshape(b, t, d), kv, norm_cross[l], wq_b[l], wo_b[l],
                             _row_tile(t, 256)).reshape(m, d)
        act = ffn_up(xf, norm_ffn[l], wgu_b[l], tm, D_FF // 2)
        xf = ffn_down(xf, act, wdn_b[l], tm)
    return final_norm(xf, norm_final, tm).reshape(b, t, d)
```

```python
import functools
import math

import jax
import jax.numpy as jnp
from jax import lax
from jax.experimental import pallas as pl
from jax.experimental.pallas import tpu as pltpu

f32 = jnp.float32
bf16 = jnp.bfloat16

D_MODEL = 1024
DEPTH = 4
HEAD_DIM = 64
BLOCK = 128
WINDOW = 128
WIN_HEADS = 4
WIN_KV_HEADS = 2
DIFF_HEADS = 4
DIFF_QK_DIM = 64
DIFF_V_DIM = 2 * DIFF_QK_DIM
RWKV_HEADS = 4
RWKV_HEAD = 64
W_LORA = 64
A_LORA = 64
G_LORA = 128
X_HEADS = 4
X_HEAD_DIM = 128
D_FF = -(-8 * D_MODEL // (3 * 256)) * 256
RMS_EPS = 1e-6
GN_EPS = 64e-5

WIN_WIDTH = WIN_HEADS * HEAD_DIM
DIFF_WIDTH = DIFF_HEADS * DIFF_V_DIM
RWKV_WIDTH = RWKV_HEADS * RWKV_HEAD
RWKV_IN = 3 * RWKV_WIDTH + W_LORA + A_LORA + G_LORA
IN_WIDTH = 2 * WIN_WIDTH + 3 * DIFF_WIDTH + RWKV_IN

OFF_AQ = 0
OFF_AK = WIN_WIDTH
OFF_AV = OFF_AK + WIN_KV_HEADS * HEAD_DIM
OFF_BQ = OFF_AV + WIN_KV_HEADS * HEAD_DIM
OFF_BK = OFF_BQ + DIFF_WIDTH
OFF_BV = OFF_BK + DIFF_WIDTH
OFF_C = OFF_BV + DIFF_WIDTH
IN_GROUP = 1024
assert OFF_BK == IN_GROUP and OFF_C == 2 * IN_GROUP and IN_WIDTH == 3 * IN_GROUP and RWKV_IN == IN_GROUP

CHUNK = 64
INV_BLOCK = 8
assert INV_BLOCK == 8 and CHUNK == 8 * INV_BLOCK
WIDE = RWKV_HEADS * CHUNK
V7X_VMEM_LIMIT = 56 * 1024 * 1024
LOG2E = 1.4426950408889634
SOFTMAX_ROWS = 16

_SLOPES = [2.0 ** (-8.0 * i / (WIN_HEADS + DIFF_HEADS)) for i in range(1, WIN_HEADS + DIFF_HEADS + 1)]
SLOPES_WIN = _SLOPES[0::2]
SLOPES_DIFF = _SLOPES[1::2]


def _params(*sem):
    return pltpu.CompilerParams(dimension_semantics=sem, vmem_limit_bytes=V7X_VMEM_LIMIT)


def _dot(a, b):
    return jnp.dot(a.astype(bf16), b.astype(bf16), preferred_element_type=f32)


def _dot_nt(a, b):
    return lax.dot_general(a.astype(bf16), b.astype(bf16), (((1,), (1,)), ((), ())),
                           preferred_element_type=f32)


def _each(fn, *lists):
    return [fn(*args) for args in zip(*lists)]


def _split2(a):
    hi = a.astype(bf16)
    lo = (a - hi.astype(f32)).astype(bf16)
    return hi, lo


def _split3(a):
    hi = a.astype(bf16)
    r1 = a - hi.astype(f32)
    mid = r1.astype(bf16)
    lo = (r1 - mid.astype(f32)).astype(bf16)
    return hi, mid, lo


def _dot3(a, b, nt=False):
    d = _dot_nt if nt else _dot
    ah, al = _split2(a)
    bh, bl = _split2(b)
    return d(ah, bh) + (d(ah, bl) + d(al, bh))


def _dot_exact_rhs(a_bf16, b):
    b1, b2, b3 = _split3(b)
    return (jnp.dot(a_bf16, b1, preferred_element_type=f32)
            + (jnp.dot(a_bf16, b2, preferred_element_type=f32)
               + jnp.dot(a_bf16, b3, preferred_element_type=f32)))


def _dot_exact_lhs(a, b_bf16):
    a1, a2, a3 = _split3(a)
    return (jnp.dot(a1, b_bf16, preferred_element_type=f32)
            + (jnp.dot(a2, b_bf16, preferred_element_type=f32)
               + jnp.dot(a3, b_bf16, preferred_element_type=f32)))


def _rms(x, g):
    ms = jnp.mean(x * x, axis=-1, keepdims=True)
    return (x * lax.rsqrt(ms + RMS_EPS)) * g


def _sigmoid(z):
    return 1.0 / (1.0 + jnp.exp(-z))


def _norm_matmul_kernel(x_ref, g_ref, w_ref, o_ref, h_ref):
    @pl.when(pl.program_id(1) == 0)
    def _():
        h_ref[...] = _rms(x_ref[...], g_ref[...]).astype(bf16)

    o_ref[...] = jnp.dot(h_ref[...], w_ref[...], preferred_element_type=f32).astype(o_ref.dtype)


def norm_matmul(x, g, w, tm, tn, out_dtype=f32):
    m, k = x.shape
    n = w.shape[1]
    return pl.pallas_call(
        _norm_matmul_kernel,
        grid=(m // tm, n // tn),
        in_specs=[pl.BlockSpec((tm, k), lambda i, j: (i, 0)),
                  pl.BlockSpec((1, k), lambda i, j: (0, 0)),
                  pl.BlockSpec((k, tn), lambda i, j: (0, j))],
        out_specs=pl.BlockSpec((tm, tn), lambda i, j: (i, j)),
        out_shape=jax.ShapeDtypeStruct((m, n), out_dtype),
        scratch_shapes=[pltpu.VMEM((tm, k), bf16)],
        compiler_params=_params("parallel", "arbitrary"),
        name="norm_matmul",
    )(x, g.reshape(1, k), w)


def _in_proj_kernel(x_ref, g_ref, w_ref, oa_ref, ob_ref, oc_ref, h_ref):
    j = pl.program_id(1)

    @pl.when(j == 0)
    def _():
        h_ref[...] = _rms(x_ref[...], g_ref[...]).astype(bf16)

    y = jnp.dot(h_ref[...], w_ref[...], preferred_element_type=f32)
    for idx, ref in enumerate((oa_ref, ob_ref, oc_ref)):
        @pl.when(j == idx)
        def _(ref=ref):
            ref[...] = y.astype(ref.dtype)


def in_proj(x, g, w, tm):
    m, k = x.shape
    assert w.shape[1] == 3 * IN_GROUP
    out = pl.BlockSpec((tm, IN_GROUP), lambda i, j: (i, 0))
    return pl.pallas_call(
        _in_proj_kernel,
        grid=(m // tm, 3),
        in_specs=[pl.BlockSpec((tm, k), lambda i, j: (i, 0)),
                  pl.BlockSpec((1, k), lambda i, j: (0, 0)),
                  pl.BlockSpec((k, IN_GROUP), lambda i, j: (0, j))],
        out_specs=[out, out, out],
        out_shape=[jax.ShapeDtypeStruct((m, IN_GROUP), f32), jax.ShapeDtypeStruct((m, IN_GROUP), bf16),
                   jax.ShapeDtypeStruct((m, IN_GROUP), f32)],
        scratch_shapes=[pltpu.VMEM((tm, k), bf16)],
        compiler_params=_params("parallel", "arbitrary"),
        name="in_proj",
    )(x, g.reshape(1, k), w)


def _win_kernel(sink_ref, q_ref, k_ref, v_ref, o_ref, *, seq, tq):
    i = pl.program_id(1)
    span = tq + 2 * BLOCK
    start = pl.multiple_of(jnp.clip(i * tq - BLOCK, 0, seq - span), BLOCK)
    kw = k_ref[0, pl.ds(start, span), :].astype(bf16)
    vw = v_ref[0, pl.ds(start, span), :].astype(bf16)
    q = (q_ref[0] * (HEAD_DIM ** -0.5 * LOG2E)).astype(bf16)
    qpos = i * tq + lax.broadcasted_iota(jnp.int32, (tq, span), 0)
    kpos = start + lax.broadcasted_iota(jnp.int32, (tq, span), 1)
    dist = jnp.abs(kpos - qpos)
    valid = dist <= WINDOW
    distf = dist.astype(f32)
    group = WIN_HEADS // WIN_KV_HEADS
    heads = list(range(WIN_HEADS))
    kv_of = lambda x, head: x[:, (head // group) * HEAD_DIM:(head // group + 1) * HEAD_DIM]
    sinks = [sink_ref[head] * LOG2E for head in heads]
    s = [_dot_nt(q[:, head * HEAD_DIM:(head + 1) * HEAD_DIM], kv_of(kw, head)) for head in heads]
    s = [jnp.where(valid, x - (SLOPES_WIN[head] * LOG2E) * distf, -jnp.inf) for head, x in zip(heads, s)]
    m = _each(lambda x, sk: jnp.maximum(jnp.max(x, axis=-1, keepdims=True), sk), s, sinks)
    e = _each(lambda x, mm: jnp.exp2(x - mm), s, m)
    denom = _each(lambda x, mm, sk: jnp.sum(x, axis=-1, keepdims=True) + jnp.exp2(sk - mm), e, m, sinks)
    pv = [jnp.dot(x.astype(bf16), kv_of(vw, head), preferred_element_type=f32) for head, x in zip(heads, e)]
    outs = _each(lambda x, dd: x / dd, pv, denom)
    o_ref[0] = jnp.concatenate(outs, axis=-1).astype(o_ref.dtype)


def window_attention(proj, sink, tq):
    b, t, _ = proj.shape
    kvw = WIN_KV_HEADS * HEAD_DIM
    return pl.pallas_call(
        functools.partial(_win_kernel, seq=t, tq=tq),
        grid=(b, t // tq),
        in_specs=[pl.BlockSpec(memory_space=pltpu.SMEM),
                  pl.BlockSpec((1, tq, WIN_WIDTH), lambda bi, i: (bi, i, OFF_AQ // WIN_WIDTH)),
                  pl.BlockSpec((1, t, kvw), lambda bi, i: (bi, 0, OFF_AK // kvw)),
                  pl.BlockSpec((1, t, kvw), lambda bi, i: (bi, 0, OFF_AV // kvw))],
        out_specs=pl.BlockSpec((1, tq, WIN_WIDTH), lambda bi, i: (bi, i, 0)),
        out_shape=jax.ShapeDtypeStruct((b, t, WIN_WIDTH), bf16),
        compiler_params=_params("parallel", "arbitrary"),
        name="window_attention",
    )(sink, proj, proj, proj)


def _diff_kernel(slope_ref, lq1_ref, lk1_ref, lq2_ref, lk2_ref, g_ref, dist_ref, q_ref, k_ref, v_ref, o_ref,
                 s_ref, e_ref, *, seq, tq, lam_init):
    h = pl.program_id(1)
    i = pl.program_id(2)
    lam = (jnp.exp(jnp.sum(lq1_ref[...] * lk1_ref[...], axis=-1, keepdims=True))
           - jnp.exp(jnp.sum(lq2_ref[...] * lk2_ref[...], axis=-1, keepdims=True)) + lam_init)
    q = q_ref[0] * (DIFF_QK_DIM ** -0.5 * LOG2E)
    k = k_ref[0]
    for c in range(2):
        s_ref[c] = _dot_nt(q[:, c * DIFF_QK_DIM:(c + 1) * DIFF_QK_DIM],
                           k[:, c * DIFF_QK_DIM:(c + 1) * DIFF_QK_DIM])
    nkb = seq // BLOCK
    first = (seq // tq - 1 - i) * (tq // BLOCK)
    slope = slope_ref[h] * LOG2E
    sums = [[], []]
    for rb in range(tq // SOFTMAX_ROWS):
        rows = slice(rb * SOFTMAX_ROWS, (rb + 1) * SOFTMAX_ROWS)
        bias = jnp.concatenate([slope * dist_ref[first + kb, rows, :] for kb in range(nkb)], axis=1)
        for c in range(2):
            s = s_ref[c, rows, :] - bias
            e = jnp.exp2(s - jnp.max(s, axis=-1, keepdims=True))
            sums[c].append(jnp.sum(e, axis=-1, keepdims=True))
            e_ref[c, rows, :] = e.astype(bf16)
    v = v_ref[0].astype(bf16)
    outs = [jnp.dot(e_ref[c], v, preferred_element_type=f32) / jnp.concatenate(sums[c], axis=0) for c in range(2)]
    o = outs[0] - lam * outs[1]
    o_ref[0] = (_rms(o, g_ref[...]) * (1.0 - lam_init)).astype(o_ref.dtype)


def _distance_table(seq, tq):
    off = (seq // tq - 1) * (tq // BLOCK)
    nj = seq // BLOCK + off
    j = lax.broadcasted_iota(jnp.int32, (nj, tq, BLOCK), 0)
    q = lax.broadcasted_iota(jnp.int32, (nj, tq, BLOCK), 1)
    kl = lax.broadcasted_iota(jnp.int32, (nj, tq, BLOCK), 2)
    return jnp.abs((j - off) * BLOCK + kl - q).astype(f32)


def diff_attention(qa, kv, lq1, lk1, lq2, lk2, subln, lam_init, tq):
    b, t, _ = qa.shape
    w = DIFF_V_DIM
    vec = pl.BlockSpec((1, DIFF_QK_DIM), lambda bi, h, i: (0, 0))
    slopes = jnp.asarray(SLOPES_DIFF, f32)
    dist = _distance_table(t, tq)
    return pl.pallas_call(
        functools.partial(_diff_kernel, seq=t, tq=tq, lam_init=lam_init),
        grid=(b, DIFF_HEADS, t // tq),
        in_specs=[pl.BlockSpec(memory_space=pltpu.SMEM), vec, vec, vec, vec,
                  pl.BlockSpec((1, w), lambda bi, h, i: (0, 0)),
                  pl.BlockSpec(dist.shape, lambda bi, h, i: (0, 0, 0)),
                  pl.BlockSpec((1, tq, w), lambda bi, h, i: (bi, i, OFF_BQ // w + h)),
                  pl.BlockSpec((1, t, w), lambda bi, h, i: (bi, 0, (OFF_BK - IN_GROUP) // w + h)),
                  pl.BlockSpec((1, t, w), lambda bi, h, i: (bi, 0, (OFF_BV - IN_GROUP) // w + h))],
        out_specs=pl.BlockSpec((1, tq, w), lambda bi, h, i: (bi, i, h)),
        out_shape=jax.ShapeDtypeStruct((b, t, DIFF_WIDTH), bf16),
        scratch_shapes=[pltpu.VMEM((2, tq, t), f32), pltpu.VMEM((2, tq, t), bf16)],
        compiler_params=_params("parallel", "parallel", "arbitrary"),
        name="diff_attention",
    )(slopes, lq1.reshape(1, -1), lk1.reshape(1, -1), lq2.reshape(1, -1), lk2.reshape(1, -1),
      subln.reshape(1, -1), dist, qa, kv, kv)


def _head_ones():
    r = lax.broadcasted_iota(jnp.int32, (RWKV_WIDTH, RWKV_WIDTH), 0) // RWKV_HEAD
    c = lax.broadcasted_iota(jnp.int32, (RWKV_WIDTH, RWKV_WIDTH), 1) // RWKV_HEAD
    return jnp.where(r == c, 1.0, 0.0).astype(bf16)


def _rwkv_prep_kernel(x_ref, xp_ref, xn_ref, conv_ref, w0_ref, wup_ref, a0_ref, aup_ref, gup_ref,
                      kk_ref, ka_ref, rk_ref,
                      r_out, k_out, v_out, kkn_out, g_out, bonus_out, lw_out, a_out, *, tt):
    i = pl.program_id(1)
    n = pl.num_programs(1)
    x = x_ref[0]
    first = jnp.where(i > 0, xp_ref[0, 7:8, :], 0.0)
    last = jnp.where(i < n - 1, xn_ref[0, 0:1, :], 0.0)
    row = lax.broadcasted_iota(jnp.int32, (tt, 1), 0)
    x_prev = jnp.where(row == 0, first, pltpu.roll(x, 1, 0))
    x_next = jnp.where(row == tt - 1, last, pltpu.roll(x, tt - 1, 0))
    c = x_prev * conv_ref[0:1, :] + x * conv_ref[1:2, :] + x_next * conv_ref[2:3, :]
    hw = RWKV_WIDTH
    r = c[:, 0:hw]
    k = c[:, hw:2 * hw]
    v = c[:, 2 * hw:3 * hw]
    o = 3 * hw
    wd = c[:, o:o + W_LORA]
    ad = c[:, o + W_LORA:o + W_LORA + A_LORA]
    gd = c[:, o + W_LORA + A_LORA:o + W_LORA + A_LORA + G_LORA]
    ones = _head_ones()
    kscaled = k * kk_ref[...]
    norm = jnp.sqrt(_dot_exact_lhs(kscaled * kscaled, ones))
    kkn = kscaled / jnp.maximum(norm, 1e-12)
    tw = jnp.tanh(wd)
    rk_sum = jnp.zeros_like(r)
    for d in range(2):
        z = w0_ref[d:d + 1, :] + _dot3(tw, wup_ref[d])
        lw_out[d, 0] = -math.exp(-0.5) * _sigmoid(z)
        a = _sigmoid(a0_ref[d:d + 1, :] + _dot3(ad, aup_ref[d]))
        a_out[d, 0] = a
        kd = k * (1.0 + (a - 1.0) * ka_ref[...])
        rk_sum = rk_sum + _dot_exact_lhs(r * kd * rk_ref[...], ones)
    r_out[0] = r
    k_out[0] = k
    v_out[0] = v
    kkn_out[0] = kkn
    g_out[0] = _dot(_sigmoid(gd), gup_ref[...])
    bonus_out[0] = rk_sum * v


def rwkv_prep(proj, conv, w0, w_up, a0, a_up, g_up, k_k, k_a, r_k, tt):
    b, t, _ = proj.shape
    hw = RWKV_WIDTH
    nblk8 = t // 8
    cb = 0
    full = lambda shape: pl.BlockSpec(shape, lambda bi, i: (0,) * len(shape))
    tok = pl.BlockSpec((1, tt, hw), lambda bi, i: (bi, i, 0))
    tok2 = pl.BlockSpec((2, 1, tt, hw), lambda bi, i: (0, bi, i, 0))
    sds = jax.ShapeDtypeStruct((b, t, hw), f32)
    sds2 = jax.ShapeDtypeStruct((2, b, t, hw), f32)
    return pl.pallas_call(
        functools.partial(_rwkv_prep_kernel, tt=tt),
        grid=(b, t // tt),
        in_specs=[pl.BlockSpec((1, tt, RWKV_IN), lambda bi, i: (bi, i, cb)),
                  pl.BlockSpec((1, 8, RWKV_IN), lambda bi, i: (bi, jnp.maximum(i * (tt // 8) - 1, 0), cb)),
                  pl.BlockSpec((1, 8, RWKV_IN), lambda bi, i: (bi, jnp.minimum((i + 1) * (tt // 8), nblk8 - 1), cb)),
                  full((3, RWKV_IN)), full((2, hw)), full((2, W_LORA, hw)), full((2, hw)),
                  full((2, A_LORA, hw)), full((G_LORA, hw)), full((1, hw)), full((1, hw)), full((1, hw))],
        out_specs=[tok, tok, tok, tok, tok, tok, tok2, tok2],
        out_shape=[sds, sds, sds, sds, sds, sds, sds2, sds2],
        compiler_params=_params("parallel", "arbitrary"),
        name="rwkv_prep",
    )(proj, proj, proj, conv, w0, w_up, a0, a_up, g_up,
      k_k.reshape(1, hw), k_a.reshape(1, hw), r_k.reshape(1, hw))


def _head_masks():
    lane_head = lax.broadcasted_iota(jnp.int32, (CHUNK, WIDE), 1) // RWKV_HEAD
    return [jnp.where(lane_head == h, 1.0, 0.0).astype(bf16) for h in range(RWKV_HEADS)]


def _stack(x, masks):
    xb = x.astype(bf16)
    return jnp.concatenate([xb * mk for mk in masks], axis=0)


def _mm(a, b, masks):
    return _dot(a, _stack(b, masks))


def _unit_triangular_inverse(ms, eye, masks):
    r = lax.broadcasted_iota(jnp.int32, eye.shape, 0) // INV_BLOCK
    c = (lax.broadcasted_iota(jnp.int32, eye.shape, 1) % CHUNK) // INV_BLOCK
    mm = lambda a, b: _mm(a, b, masks)
    d = [jnp.where(r == c, m, 0.0) for m in ms]
    off = _each(lambda m, dd: m - dd, ms, d)
    x = [-dd for dd in d]
    x2 = _each(lambda xx: mm(xx, xx), x)
    x4 = _each(lambda xx: mm(xx, xx), x2)
    p1 = _each(lambda u, w: mm(eye + u, eye + w), x, x2)
    td = _each(lambda u, w: mm(u, eye + w), p1, x4)
    z = _each(mm, td, off)
    z2 = _each(lambda zz: mm(zz, zz), z)
    z4 = _each(lambda zz: mm(zz, zz), z2)
    p3 = _each(lambda u, w: mm(eye - u, eye + w), z, z2)
    p4 = _each(lambda w, t: mm(eye + w, t), z4, td)
    return _each(mm, p3, p4)


def _rwkv_chunks(revs, lws, a_s, ks, kkns, vs, rs, k_a, h_refs):
    masks = _head_masks()
    ci = lax.broadcasted_iota(jnp.int32, (CHUNK, CHUNK), 0)
    cj = lax.broadcasted_iota(jnp.int32, (CHUNK, CHUNK), 1)
    tri_f = jnp.where(cj <= ci, 1.0, 0.0).astype(bf16)
    tri_r = jnp.where(cj >= ci, 1.0, 0.0).astype(bf16)
    ti = lax.broadcasted_iota(jnp.int32, (CHUNK, WIDE), 0)
    tj = lax.broadcasted_iota(jnp.int32, (CHUNK, WIDE), 1) % CHUNK
    eye = jnp.where(ti == tj, 1.0, 0.0)
    strict = [(tj > ti) if rev else (tj < ti) for rev in revs]
    incl = [(tj >= ti) if rev else (tj <= ti) for rev in revs]

    g_incl = [_dot_exact_rhs(tri_r if rev else tri_f, lw) for rev, lw in zip(revs, lws)]
    g_tot = [g[0:1, :] if rev else g[CHUNK - 1:CHUNK, :] for rev, g in zip(revs, g_incl)]
    kat = _each(lambda kkn, a: kkn * a, kkns, a_s)
    kd = _each(lambda k, a: k * (1.0 + (a - 1.0) * k_a), ks, a_s)
    e_neg = [jnp.exp(-g) for g in g_incl]
    e_rem = _each(lambda gt, g: jnp.exp(gt - g), g_tot, g_incl)
    a_t = _each(lambda kkn, g, lw: kkn * jnp.exp(g - lw), kkns, g_incl, lws)
    r_t = _each(lambda r, g: r * jnp.exp(g), rs, g_incl)
    b_t = _each(lambda x, e: x * e, kat, e_neg)
    k_t = _each(lambda x, e: x * e, kd, e_neg)
    b_hat = _each(lambda x, e: x * e, kat, e_rem)
    k_hat = _each(lambda x, e: x * e, kd, e_rem)

    ar = _each(lambda x, y: jnp.concatenate([x, y], axis=0).astype(bf16), a_t, r_t)
    s_b = _each(lambda x, y: _dot_nt(x, _stack(y, masks)), ar, b_t)
    s_k = _each(lambda x, y: _dot_nt(x, _stack(y, masks)), ar, k_t)
    m_ab = _each(lambda s, x: jnp.where(s, x[:CHUNK], 0.0), strict, s_b)
    a_rb = _each(lambda s, x: jnp.where(s, x[CHUNK:], 0.0), incl, s_b)
    m_ak = _each(lambda s, x: jnp.where(s, x[:CHUNK], 0.0), strict, s_k)
    a_rk = _each(lambda s, x: jnp.where(s, x[CHUNK:], 0.0), incl, s_k)
    t_inv = _unit_triangular_inverse(m_ab, eye, masks)

    wi = lax.broadcasted_iota(jnp.int32, (WIDE, WIDE), 0)
    wj = lax.broadcasted_iota(jnp.int32, (WIDE, WIDE), 1)
    same_head = (wi // RWKV_HEAD) == (wj // RWKV_HEAD)
    diag = wi == wj
    h = [h_ref[...].astype(bf16) for h_ref in h_refs]
    decay = [jnp.where(diag, jnp.exp(gt), 0.0).astype(bf16) for gt in g_tot]
    arh = _each(lambda x, dd, hh: _dot(jnp.concatenate([x, dd], axis=0), hh), ar, decay, h)
    v_s = _each(lambda x: _stack(x, masks), vs)
    mkv = _each(lambda m, n, vv: _dot(jnp.concatenate([m, n], axis=0), vv), m_ak, a_rk, v_s)
    u = _each(lambda t, x, y: -_mm(t, x[:CHUNK] + y[:CHUNK], masks), t_inv, arh, mkv)
    ys = _each(lambda x, m, uu, y: x[CHUNK:2 * CHUNK] + _mm(m, uu, masks) + y[CHUNK:], arh, a_rb, u, mkv)
    bk_t = _each(lambda x, y: jnp.concatenate([x, y], axis=0).T, b_hat, k_hat)
    for h_ref, x, bk, uu, vv in zip(h_refs, arh, bk_t, u, vs):
        upd = _dot(bk, jnp.concatenate([uu, vv], axis=0))
        h_ref[...] = x[2 * CHUNK:] + jnp.where(same_head, upd, 0.0)
    return ys


def _rwkv_scan_kernel(ka_ref,
                      lw0, a0, k0, kk0, v0, r0,
                      lw1, a1, k1, kk1, v1, r1,
                      y0_ref, y1_ref, h_ref, *, rows):
    @pl.when(pl.program_id(1) == 0)
    def _():
        h_ref[...] = jnp.zeros_like(h_ref)

    revs = [False] * rows + [True] * rows
    gather = lambda f, b: [f[0, g] for g in range(rows)] + [b[0, g] for g in range(rows)]
    plain = lambda f, b: [f[g] for g in range(rows)] + [b[g] for g in range(rows)]
    ys = _rwkv_chunks(revs, gather(lw0, lw1), gather(a0, a1), plain(k0, k1), plain(kk0, kk1),
                      plain(v0, v1), plain(r0, r1), ka_ref[...], [h_ref.at[i] for i in range(2 * rows)])
    for g in range(rows):
        y0_ref[g] = ys[g]
        y1_ref[g] = ys[rows + g]


def rwkv_scan(lw, a, k, kkn, v, r, k_a, rows):
    b, t, hw = k.shape
    nc = t // CHUNK
    fwd = pl.BlockSpec((rows, CHUNK, hw), lambda bi, c: (bi, c, 0))
    bwd = pl.BlockSpec((rows, CHUNK, hw), lambda bi, c: (bi, nc - 1 - c, 0))
    fwd2 = pl.BlockSpec((1, rows, CHUNK, hw), lambda bi, c: (0, bi, c, 0))
    bwd2 = pl.BlockSpec((1, rows, CHUNK, hw), lambda bi, c: (1, bi, nc - 1 - c, 0))
    sds = jax.ShapeDtypeStruct((b, t, hw), f32)
    return pl.pallas_call(
        functools.partial(_rwkv_scan_kernel, rows=rows),
        grid=(b // rows, nc),
        in_specs=[pl.BlockSpec((1, hw), lambda bi, c: (0, 0)),
                  fwd2, fwd2, fwd, fwd, fwd, fwd,
                  bwd2, bwd2, bwd, bwd, bwd, bwd],
        out_specs=[fwd, bwd],
        out_shape=[sds, sds],
        scratch_shapes=[pltpu.VMEM((2 * rows, WIDE, WIDE), f32)],
        compiler_params=_params("parallel", "arbitrary"),
        name="rwkv_scan",
    )(k_a.reshape(1, hw), lw, a, k, kkn, v, r, lw, a, k, kkn, v, r)


def _rwkv_finish(y0, y1, g, bonus, ln_g, ln_b):
    ones = _head_ones()
    y = y0 + y1
    mu = _dot_exact_lhs(y, ones) * (1.0 / RWKV_HEAD)
    yc = y - mu
    var = _dot_exact_lhs(yc * yc, ones) * (1.0 / RWKV_HEAD)
    yn = yc * lax.rsqrt(var + GN_EPS) * ln_g + ln_b
    return (yn + bonus) * g


def _out_proj_kernel(x_ref, a_ref, b_ref, y0_ref, y1_ref, g_ref, bonus_ref, lng_ref, lnb_ref,
                     wa_ref, wb_ref, wc_ref, o_ref):
    c = _rwkv_finish(y0_ref[...], y1_ref[...], g_ref[...], bonus_ref[...], lng_ref[...], lnb_ref[...])
    acc = jnp.dot(a_ref[...], wa_ref[...], preferred_element_type=f32)
    acc = acc + jnp.dot(b_ref[...], wb_ref[...], preferred_element_type=f32)
    acc = acc + jnp.dot(c.astype(bf16), wc_ref[...], preferred_element_type=f32)
    o_ref[...] = x_ref[...] + acc


def out_proj(x, o_win, o_diff, y0, y1, g, bonus, ln_g, ln_b, w_out, tm):
    m, d = x.shape
    hw = RWKV_WIDTH
    wa = w_out[:WIN_WIDTH]
    wb = w_out[WIN_WIDTH:WIN_WIDTH + DIFF_WIDTH]
    wc = w_out[WIN_WIDTH + DIFF_WIDTH:]
    row = lambda w: pl.BlockSpec((tm, w), lambda i: (i, 0))
    full = lambda w: pl.BlockSpec((w, d), lambda i: (0, 0))
    vec = pl.BlockSpec((1, hw), lambda i: (0, 0))
    return pl.pallas_call(
        _out_proj_kernel,
        grid=(m // tm,),
        in_specs=[row(d), row(WIN_WIDTH), row(DIFF_WIDTH), row(hw), row(hw), row(hw), row(hw), vec, vec,
                  full(WIN_WIDTH), full(DIFF_WIDTH), full(hw)],
        out_specs=row(d),
        out_shape=jax.ShapeDtypeStruct((m, d), f32),
        compiler_params=_params("parallel"),
        name="out_proj",
    )(x, o_win, o_diff, y0, y1, g, bonus, ln_g.reshape(1, hw), ln_b.reshape(1, hw), wa, wb, wc)


def _cross_kernel(x_ref, g_ref, wq_ref, kv_ref, wo_ref, o_ref):
    x = x_ref[0]
    q = jnp.dot(_rms(x, g_ref[...]).astype(bf16), wq_ref[...], preferred_element_type=f32)
    kv = kv_ref[0]
    xw = X_HEADS * X_HEAD_DIM
    outs = []
    for h in range(X_HEADS):
        qh = q[:, h * X_HEAD_DIM:(h + 1) * X_HEAD_DIM]
        kh = kv[:, h * X_HEAD_DIM:(h + 1) * X_HEAD_DIM]
        vh = kv[:, xw + h * X_HEAD_DIM:xw + (h + 1) * X_HEAD_DIM]
        s = _dot_nt(qh, kh) * (X_HEAD_DIM ** -0.5)
        e = jnp.exp(s - jnp.max(s, axis=-1, keepdims=True))
        p = e / jnp.sum(e, axis=-1, keepdims=True)
        outs.append(_dot(p, vh))
    o = jnp.concatenate(outs, axis=-1)
    o_ref[0] = x + jnp.dot(o.astype(bf16), wo_ref[...], preferred_element_type=f32)


def cross_attention(x, kv, g, wq, wo, tm):
    b, t, d = x.shape
    nm = kv.shape[1]
    xw = X_HEADS * X_HEAD_DIM
    return pl.pallas_call(
        _cross_kernel,
        grid=(b, t // tm),
        in_specs=[pl.BlockSpec((1, tm, d), lambda bi, i: (bi, i, 0)),
                  pl.BlockSpec((1, d), lambda bi, i: (0, 0)),
                  pl.BlockSpec((d, xw), lambda bi, i: (0, 0)),
                  pl.BlockSpec((1, nm, 2 * xw), lambda bi, i: (bi, 0, 0)),
                  pl.BlockSpec((xw, d), lambda bi, i: (0, 0))],
        out_specs=pl.BlockSpec((1, tm, d), lambda bi, i: (bi, i, 0)),
        out_shape=jax.ShapeDtypeStruct((b, t, d), f32),
        compiler_params=_params("parallel", "parallel"),
        name="cross_attention",
    )(x, g.reshape(1, d), wq, kv, wo)


def _ffn_up_kernel(x_ref, g_ref, wg_ref, wu_ref, o_ref, h_ref):
    @pl.when(pl.program_id(1) == 0)
    def _():
        h_ref[...] = _rms(x_ref[...], g_ref[...]).astype(bf16)

    h = h_ref[...]
    gate = jnp.dot(h, wg_ref[...], preferred_element_type=f32)
    up = jnp.dot(h, wu_ref[...], preferred_element_type=f32)
    o_ref[...] = (gate * _sigmoid(gate) * up).astype(o_ref.dtype)


def ffn_up(x, g, w_gu, tm, tf):
    m, d = x.shape
    nf = D_FF // tf
    return pl.pallas_call(
        _ffn_up_kernel,
        grid=(m // tm, nf),
        in_specs=[pl.BlockSpec((tm, d), lambda i, j: (i, 0)),
                  pl.BlockSpec((1, d), lambda i, j: (0, 0)),
                  pl.BlockSpec((d, tf), lambda i, j: (0, j)),
                  pl.BlockSpec((d, tf), lambda i, j: (0, j + nf))],
        out_specs=pl.BlockSpec((tm, tf), lambda i, j: (i, j)),
        out_shape=jax.ShapeDtypeStruct((m, D_FF), bf16),
        scratch_shapes=[pltpu.VMEM((tm, d), bf16)],
        compiler_params=_params("parallel", "arbitrary"),
        name="ffn_up",
    )(x, g.reshape(1, d), w_gu, w_gu)


def _ffn_down_kernel(x_ref, a_ref, w_ref, o_ref):
    o_ref[...] = x_ref[...] + jnp.dot(a_ref[...], w_ref[...], preferred_element_type=f32)


def _ffn_down_norm_kernel(x_ref, a_ref, w_ref, g_ref, o_ref):
    y = x_ref[...] + jnp.dot(a_ref[...], w_ref[...], preferred_element_type=f32)
    o_ref[...] = _rms(y, g_ref[...])


def ffn_down(x, act, w_down, tm, final_gain=None):
    m, d = x.shape
    kf = act.shape[1]
    in_specs = [pl.BlockSpec((tm, d), lambda i: (i, 0)),
                pl.BlockSpec((tm, kf), lambda i: (i, 0)),
                pl.BlockSpec((kf, d), lambda i: (0, 0))]
    args = [x, act, w_down]
    if final_gain is not None:
        in_specs.append(pl.BlockSpec((1, d), lambda i: (0, 0)))
        args.append(final_gain.reshape(1, d))
    return pl.pallas_call(
        _ffn_down_kernel if final_gain is None else _ffn_down_norm_kernel,
        grid=(m // tm,),
        in_specs=in_specs,
        out_specs=pl.BlockSpec((tm, d), lambda i: (i, 0)),
        out_shape=jax.ShapeDtypeStruct((m, d), f32),
        compiler_params=_params("parallel"),
        name="ffn_down",
    )(*args)


def _row_tile(m, want):
    t = min(want, m)
    while m % t:
        t //= 2
    return t


def kernel(x, mem, norm_mix, w_in, win_sink, diff_lq1, diff_lk1, diff_lq2, diff_lk2, diff_subln,
           rwkv_conv, rwkv_w0, rwkv_w_up, rwkv_a0, rwkv_a_up, rwkv_g_up, rwkv_k_k, rwkv_k_a, rwkv_r_k,
           rwkv_ln_g, rwkv_ln_b, w_out, norm_cross, norm_mem, x_wq, x_wkv, x_wo, norm_ffn,
           ffn_w_gu, ffn_w_down, norm_final):
    b, t, d = x.shape
    nm = mem.shape[1]
    m = b * t
    tm = _row_tile(m, 1024)
    depth = w_in.shape[0]
    w_in_b, w_out_b, wq_b, wkv_b, wo_b, wgu_b, wdn_b = (
        z.astype(bf16) for z in (w_in, w_out, x_wq, x_wkv, x_wo, ffn_w_gu, ffn_w_down))
    mem2 = mem.reshape(b * nm, d)
    xf = x.reshape(m, d)
    for l in range(depth):
        qa, kvb, cin = (z.reshape(b, t, IN_GROUP) for z in in_proj(xf, norm_mix[l], w_in_b[l], tm))
        o_win = window_attention(qa, win_sink[l], _row_tile(t, 256))
        lam_init = 0.8 - 0.6 * math.exp(-0.3 * l)
        o_diff = diff_attention(qa, kvb, diff_lq1[l], diff_lk1[l], diff_lq2[l], diff_lk2[l], diff_subln[l],
                                lam_init, _row_tile(t, 256))
        r, k, v, kkn, g, bonus, lw, a = rwkv_prep(
            cin, rwkv_conv[l], rwkv_w0[l], rwkv_w_up[l], rwkv_a0[l], rwkv_a_up[l], rwkv_g_up[l],
            rwkv_k_k[l], rwkv_k_a[l], rwkv_r_k[l], _row_tile(t, 256))
        y0, y1 = rwkv_scan(lw, a, k, kkn, v, r, rwkv_k_a[l], 4 if b % 4 == 0 else 1)
        flat = lambda z: z.reshape(m, z.shape[-1])
        xf = out_proj(xf, flat(o_win), flat(o_diff), flat(y0), flat(y1), flat(g), flat(bonus),
                      rwkv_ln_g[l], rwkv_ln_b[l], w_out_b[l], tm)
        kv = norm_matmul(mem2, norm_mem, wkv_b[l], _row_tile(b * nm, 512), 1024).reshape(b, nm, -1)
        xf = cross_attention(xf.reshape(b, t, d), kv, norm_cross[l], wq_b[l], wo_b[l],
                             _row_tile(t, 256)).reshape(m, d)
        act = ffn_up(xf, norm_ffn[l], wgu_b[l], tm, D_FF // 2)
        xf = ffn_down(xf, act, wdn_b[l], tm, norm_final if l == depth - 1 else None)
    return xf.reshape(b, t, d)
```

```python
import functools
import math

import jax
import jax.numpy as jnp
from jax import lax
from jax.experimental import pallas as pl
from jax.experimental.pallas import tpu as pltpu

f32 = jnp.float32
bf16 = jnp.bfloat16

D_MODEL = 1024
DEPTH = 4
HEAD_DIM = 64
BLOCK = 128
WINDOW = 128
WIN_HEADS = 4
WIN_KV_HEADS = 2
DIFF_HEADS = 4
DIFF_QK_DIM = 64
DIFF_V_DIM = 2 * DIFF_QK_DIM
RWKV_HEADS = 4
RWKV_HEAD = 64
W_LORA = 64
A_LORA = 64
G_LORA = 128
X_HEADS = 4
X_HEAD_DIM = 128
D_FF = -(-8 * D_MODEL // (3 * 256)) * 256
RMS_EPS = 1e-6
GN_EPS = 64e-5

WIN_WIDTH = WIN_HEADS * HEAD_DIM
DIFF_WIDTH = DIFF_HEADS * DIFF_V_DIM
RWKV_WIDTH = RWKV_HEADS * RWKV_HEAD
RWKV_IN = 3 * RWKV_WIDTH + W_LORA + A_LORA + G_LORA
IN_WIDTH = 2 * WIN_WIDTH + 3 * DIFF_WIDTH + RWKV_IN

OFF_AQ = 0
OFF_AK = WIN_WIDTH
OFF_AV = OFF_AK + WIN_KV_HEADS * HEAD_DIM
OFF_BQ = OFF_AV + WIN_KV_HEADS * HEAD_DIM
OFF_BK = OFF_BQ + DIFF_WIDTH
OFF_BV = OFF_BK + DIFF_WIDTH
OFF_C = OFF_BV + DIFF_WIDTH
IN_GROUP = 1024
assert OFF_BK == IN_GROUP and OFF_C == 2 * IN_GROUP and IN_WIDTH == 3 * IN_GROUP and RWKV_IN == IN_GROUP

CHUNK = 64
INV_BLOCK = 8
assert INV_BLOCK == 8 and CHUNK == 8 * INV_BLOCK
WIDE = RWKV_HEADS * CHUNK
V7X_VMEM_LIMIT = 56 * 1024 * 1024
LOG2E = 1.4426950408889634
SOFTMAX_ROWS = 16

_SLOPES = [2.0 ** (-8.0 * i / (WIN_HEADS + DIFF_HEADS)) for i in range(1, WIN_HEADS + DIFF_HEADS + 1)]
SLOPES_WIN = _SLOPES[0::2]
SLOPES_DIFF = _SLOPES[1::2]


def _params(*sem):
    return pltpu.CompilerParams(dimension_semantics=sem, vmem_limit_bytes=V7X_VMEM_LIMIT)


def _dot(a, b):
    return jnp.dot(a.astype(bf16), b.astype(bf16), preferred_element_type=f32)


def _dot_nt(a, b):
    return lax.dot_general(a.astype(bf16), b.astype(bf16), (((1,), (1,)), ((), ())),
                           preferred_element_type=f32)


def _each(fn, *lists):
    return [fn(*args) for args in zip(*lists)]


def _split2(a):
    hi = a.astype(bf16)
    lo = (a - hi.astype(f32)).astype(bf16)
    return hi, lo


def _split3(a):
    hi = a.astype(bf16)
    r1 = a - hi.astype(f32)
    mid = r1.astype(bf16)
    lo = (r1 - mid.astype(f32)).astype(bf16)
    return hi, mid, lo


def _dot3(a, b, nt=False):
    d = _dot_nt if nt else _dot
    ah, al = _split2(a)
    bh, bl = _split2(b)
    return d(ah, bh) + (d(ah, bl) + d(al, bh))


def _dot_exact_rhs(a_bf16, b):
    b1, b2, b3 = _split3(b)
    return (jnp.dot(a_bf16, b1, preferred_element_type=f32)
            + (jnp.dot(a_bf16, b2, preferred_element_type=f32)
               + jnp.dot(a_bf16, b3, preferred_element_type=f32)))


def _dot_exact_lhs(a, b_bf16):
    a1, a2, a3 = _split3(a)
    return (jnp.dot(a1, b_bf16, preferred_element_type=f32)
            + (jnp.dot(a2, b_bf16, preferred_element_type=f32)
               + jnp.dot(a3, b_bf16, preferred_element_type=f32)))


def _rms(x, g):
    ms = jnp.mean(x * x, axis=-1, keepdims=True)
    return (x * lax.rsqrt(ms + RMS_EPS)) * g


def _sigmoid(z):
    return 1.0 / (1.0 + jnp.exp(-z))


def _norm_matmul_kernel(x_ref, g_ref, w_ref, o_ref, h_ref):
    @pl.when(pl.program_id(1) == 0)
    def _():
        h_ref[...] = _rms(x_ref[...], g_ref[...]).astype(bf16)

    o_ref[...] = jnp.dot(h_ref[...], w_ref[...], preferred_element_type=f32).astype(o_ref.dtype)


def norm_matmul(x, g, w, tm, tn, out_dtype=f32):
    m, k = x.shape
    n = w.shape[1]
    return pl.pallas_call(
        _norm_matmul_kernel,
        grid=(m // tm, n // tn),
        in_specs=[pl.BlockSpec((tm, k), lambda i, j: (i, 0)),
                  pl.BlockSpec((1, k), lambda i, j: (0, 0)),
                  pl.BlockSpec((k, tn), lambda i, j: (0, j))],
        out_specs=pl.BlockSpec((tm, tn), lambda i, j: (i, j)),
        out_shape=jax.ShapeDtypeStruct((m, n), out_dtype),
        scratch_shapes=[pltpu.VMEM((tm, k), bf16)],
        compiler_params=_params("parallel", "arbitrary"),
        name="norm_matmul",
    )(x, g.reshape(1, k), w)


def _in_proj_kernel(x_ref, g_ref, w_ref, oa_ref, ob_ref, oc_ref, h_ref):
    j = pl.program_id(1)

    @pl.when(j == 0)
    def _():
        h_ref[...] = _rms(x_ref[...], g_ref[...]).astype(bf16)

    y = jnp.dot(h_ref[...], w_ref[...], preferred_element_type=f32)
    for idx, ref in enumerate((oa_ref, ob_ref, oc_ref)):
        @pl.when(j == idx)
        def _(ref=ref):
            ref[...] = y.astype(ref.dtype)


def in_proj(x, g, w, tm):
    m, k = x.shape
    assert w.shape[1] == 3 * IN_GROUP
    out = pl.BlockSpec((tm, IN_GROUP), lambda i, j: (i, 0))
    return pl.pallas_call(
        _in_proj_kernel,
        grid=(m // tm, 3),
        in_specs=[pl.BlockSpec((tm, k), lambda i, j: (i, 0)),
                  pl.BlockSpec((1, k), lambda i, j: (0, 0)),
                  pl.BlockSpec((k, IN_GROUP), lambda i, j: (0, j))],
        out_specs=[out, out, out],
        out_shape=[jax.ShapeDtypeStruct((m, IN_GROUP), f32), jax.ShapeDtypeStruct((m, IN_GROUP), bf16),
                   jax.ShapeDtypeStruct((m, IN_GROUP), f32)],
        scratch_shapes=[pltpu.VMEM((tm, k), bf16)],
        compiler_params=_params("parallel", "arbitrary"),
        name="in_proj",
    )(x, g.reshape(1, k), w)


def _win_kernel(sink_ref, q_ref, k_ref, v_ref, o_ref, *, seq, tq):
    i = pl.program_id(1)
    span = tq + 2 * BLOCK
    start = pl.multiple_of(jnp.clip(i * tq - BLOCK, 0, seq - span), BLOCK)
    kw = k_ref[0, pl.ds(start, span), :].astype(bf16)
    vw = v_ref[0, pl.ds(start, span), :].astype(bf16)
    q = (q_ref[0] * (HEAD_DIM ** -0.5 * LOG2E)).astype(bf16)
    qpos = i * tq + lax.broadcasted_iota(jnp.int32, (tq, span), 0)
    kpos = start + lax.broadcasted_iota(jnp.int32, (tq, span), 1)
    dist = jnp.abs(kpos - qpos)
    valid = dist <= WINDOW
    distf = dist.astype(f32)
    group = WIN_HEADS // WIN_KV_HEADS
    heads = list(range(WIN_HEADS))
    kv_of = lambda x, head: x[:, (head // group) * HEAD_DIM:(head // group + 1) * HEAD_DIM]
    sinks = [sink_ref[head] * LOG2E for head in heads]
    s = [_dot_nt(q[:, head * HEAD_DIM:(head + 1) * HEAD_DIM], kv_of(kw, head)) for head in heads]
    s = [jnp.where(valid, x - (SLOPES_WIN[head] * LOG2E) * distf, -jnp.inf) for head, x in zip(heads, s)]
    m = _each(lambda x, sk: jnp.maximum(jnp.max(x, axis=-1, keepdims=True), sk), s, sinks)
    e = _each(lambda x, mm: jnp.exp2(x - mm), s, m)
    denom = _each(lambda x, mm, sk: jnp.sum(x, axis=-1, keepdims=True) + jnp.exp2(sk - mm), e, m, sinks)
    pv = [jnp.dot(x.astype(bf16), kv_of(vw, head), preferred_element_type=f32) for head, x in zip(heads, e)]
    outs = _each(lambda x, dd: x / dd, pv, denom)
    o_ref[0] = jnp.concatenate(outs, axis=-1).astype(o_ref.dtype)


def window_attention(proj, sink, tq):
    b, t, _ = proj.shape
    kvw = WIN_KV_HEADS * HEAD_DIM
    return pl.pallas_call(
        functools.partial(_win_kernel, seq=t, tq=tq),
        grid=(b, t // tq),
        in_specs=[pl.BlockSpec(memory_space=pltpu.SMEM),
                  pl.BlockSpec((1, tq, WIN_WIDTH), lambda bi, i: (bi, i, OFF_AQ // WIN_WIDTH)),
                  pl.BlockSpec((1, t, kvw), lambda bi, i: (bi, 0, OFF_AK // kvw)),
                  pl.BlockSpec((1, t, kvw), lambda bi, i: (bi, 0, OFF_AV // kvw))],
        out_specs=pl.BlockSpec((1, tq, WIN_WIDTH), lambda bi, i: (bi, i, 0)),
        out_shape=jax.ShapeDtypeStruct((b, t, WIN_WIDTH), bf16),
        compiler_params=_params("parallel", "arbitrary"),
        name="window_attention",
    )(sink, proj, proj, proj)


def _diff_kernel(slope_ref, lq1_ref, lk1_ref, lq2_ref, lk2_ref, g_ref, dist_ref, q_ref, k_ref, v_ref, o_ref,
                 s_ref, e_ref, vt_ref, mx_ref, *, seq, tq, lam_init):
    h = pl.program_id(1)
    i = pl.program_id(2)
    nq = seq // tq
    nkb = seq // BLOCK
    groups = BLOCK // 8
    kt = 2 * BLOCK

    @pl.when(i == 0)
    def _():
        r = lax.broadcasted_iota(jnp.int32, (DIFF_V_DIM, DIFF_V_DIM), 0)
        c = lax.broadcasted_iota(jnp.int32, (DIFF_V_DIM, DIFF_V_DIM), 1)
        eye = jnp.where(r == c, 1.0, 0.0).astype(bf16)
        vt_ref[...] = _dot_nt(eye, v_ref[0]).astype(bf16)

    def step(do_a, do_b, slot_a):
        slot_b = 1 - slot_a
        if do_a:
            q = q_ref[0] * (DIFF_QK_DIM ** -0.5 * LOG2E)
            lane = lax.broadcasted_iota(jnp.int32, q.shape, 1)
            qm = [jnp.where((lane // DIFF_QK_DIM) == c, q, 0.0).astype(bf16) for c in range(2)]
            first = (nq - 1 - i) * (tq // BLOCK)
            slope = slope_ref[h] * LOG2E
            mx = [None, None]
        if do_b:
            m = [jnp.max(mx_ref[slot_b, c], axis=0, keepdims=True) for c in range(2)]
            sm = [None, None]
            acc = [None, None]
        for kb in range(nkb):
            rows = slice(kb * BLOCK, (kb + 1) * BLOCK)
            if do_a:
                kblk = k_ref[0, rows, :]
                bias = slope * dist_ref[first + kb]
                for c in range(2):
                    s = _dot_nt(kblk, qm[c]) - bias
                    s_ref[slot_a, c, rows, :] = s
                    part = jnp.max(s.reshape(groups, 8, tq), axis=0)
                    mx[c] = part if mx[c] is None else jnp.maximum(mx[c], part)
            if do_b:
                for c in range(2):
                    e = jnp.exp2(s_ref[slot_b, c, rows, :] - m[c])
                    e_ref[c, rows, :] = e.astype(bf16)
                    part = jnp.sum(e.reshape(groups, 8, tq), axis=0)
                    sm[c] = part if sm[c] is None else sm[c] + part
                if (kb + 1) % (kt // BLOCK) == 0:
                    keys = slice((kb + 1) * BLOCK - kt, (kb + 1) * BLOCK)
                    for c in range(2):
                        part = jnp.dot(vt_ref[:, keys], e_ref[c, keys, :], preferred_element_type=f32)
                        acc[c] = part if acc[c] is None else acc[c] + part
        if do_a:
            for c in range(2):
                mx_ref[slot_a, c] = mx[c]
        if do_b:
            lam = (jnp.exp(jnp.sum(lq1_ref[...] * lk1_ref[...], axis=-1, keepdims=True))
                   - jnp.exp(jnp.sum(lq2_ref[...] * lk2_ref[...], axis=-1, keepdims=True)) + lam_init)
            outs = [acc[c] / jnp.sum(sm[c], axis=0, keepdims=True) for c in range(2)]
            ot = outs[0] - lam * outs[1]
            ms = jnp.mean(ot * ot, axis=0, keepdims=True)
            ot = ot * lax.rsqrt(ms + RMS_EPS) * g_ref[...] * (1.0 - lam_init)
            o_ref[0] = ot.T.astype(o_ref.dtype)

    middle = (i > 0) & (i < nq)
    pl.when(i == 0)(lambda: step(True, False, 0))
    pl.when(middle & (i % 2 == 0))(lambda: step(True, True, 0))
    pl.when(middle & (i % 2 == 1))(lambda: step(True, True, 1))
    pl.when(i == nq)(lambda: step(False, True, nq % 2))


def _distance_table(seq, tq):
    off = (seq // tq - 1) * (tq // BLOCK)
    nj = seq // BLOCK + off
    j = lax.broadcasted_iota(jnp.int32, (nj, BLOCK, tq), 0)
    kl = lax.broadcasted_iota(jnp.int32, (nj, BLOCK, tq), 1)
    q = lax.broadcasted_iota(jnp.int32, (nj, BLOCK, tq), 2)
    return jnp.abs((j - off) * BLOCK + kl - q).astype(f32)


def diff_attention(qa, kv, lq1, lk1, lq2, lk2, subln, lam_init, tq):
    b, t, _ = qa.shape
    w = DIFF_V_DIM
    vec = pl.BlockSpec((1, DIFF_QK_DIM), lambda bi, h, i: (0, 0))
    slopes = jnp.asarray(SLOPES_DIFF, f32)
    dist = _distance_table(t, tq)
    nq = t // tq
    return pl.pallas_call(
        functools.partial(_diff_kernel, seq=t, tq=tq, lam_init=lam_init),
        grid=(b, DIFF_HEADS, nq + 1),
        in_specs=[pl.BlockSpec(memory_space=pltpu.SMEM), vec, vec, vec, vec,
                  pl.BlockSpec((w, 1), lambda bi, h, i: (0, 0)),
                  pl.BlockSpec(dist.shape, lambda bi, h, i: (0, 0, 0)),
                  pl.BlockSpec((1, tq, w), lambda bi, h, i: (bi, jnp.minimum(i, nq - 1), OFF_BQ // w + h)),
                  pl.BlockSpec((1, t, w), lambda bi, h, i: (bi, 0, (OFF_BK - IN_GROUP) // w + h)),
                  pl.BlockSpec((1, t, w), lambda bi, h, i: (bi, 0, (OFF_BV - IN_GROUP) // w + h))],
        out_specs=pl.BlockSpec((1, tq, w), lambda bi, h, i: (bi, jnp.maximum(i - 1, 0), h)),
        out_shape=jax.ShapeDtypeStruct((b, t, DIFF_WIDTH), bf16),
        scratch_shapes=[pltpu.VMEM((2, 2, t, tq), f32), pltpu.VMEM((2, t, tq), bf16), pltpu.VMEM((w, t), bf16),
                        pltpu.VMEM((2, 2, 8, tq), f32)],
        compiler_params=_params("parallel", "parallel", "arbitrary"),
        name="diff_attention",
    )(slopes, lq1.reshape(1, -1), lk1.reshape(1, -1), lq2.reshape(1, -1), lk2.reshape(1, -1),
      subln.reshape(-1, 1), dist, qa, kv, kv)


def _head_ones():
    r = lax.broadcasted_iota(jnp.int32, (RWKV_WIDTH, RWKV_WIDTH), 0) // RWKV_HEAD
    c = lax.broadcasted_iota(jnp.int32, (RWKV_WIDTH, RWKV_WIDTH), 1) // RWKV_HEAD
    return jnp.where(r == c, 1.0, 0.0).astype(bf16)


def _rwkv_prep_kernel(x_ref, xp_ref, xn_ref, conv_ref, w0_ref, wup_ref, a0_ref, aup_ref, gup_ref,
                      kk_ref, ka_ref, rk_ref,
                      r_out, k_out, v_out, kkn_out, g_out, bonus_out, lw_out, a_out, *, tt):
    i = pl.program_id(1)
    n = pl.num_programs(1)
    x = x_ref[0]
    first = jnp.where(i > 0, xp_ref[0, 7:8, :], 0.0)
    last = jnp.where(i < n - 1, xn_ref[0, 0:1, :], 0.0)
    row = lax.broadcasted_iota(jnp.int32, (tt, 1), 0)
    x_prev = jnp.where(row == 0, first, pltpu.roll(x, 1, 0))
    x_next = jnp.where(row == tt - 1, last, pltpu.roll(x, tt - 1, 0))
    c = x_prev * conv_ref[0:1, :] + x * conv_ref[1:2, :] + x_next * conv_ref[2:3, :]
    hw = RWKV_WIDTH
    r = c[:, 0:hw]
    k = c[:, hw:2 * hw]
    v = c[:, 2 * hw:3 * hw]
    o = 3 * hw
    wd = c[:, o:o + W_LORA]
    ad = c[:, o + W_LORA:o + W_LORA + A_LORA]
    gd = c[:, o + W_LORA + A_LORA:o + W_LORA + A_LORA + G_LORA]
    ones = _head_ones()
    kscaled = k * kk_ref[...]
    norm = jnp.sqrt(_dot_exact_lhs(kscaled * kscaled, ones))
    kkn = kscaled / jnp.maximum(norm, 1e-12)
    tw = jnp.tanh(wd)
    rk_sum = jnp.zeros_like(r)
    for d in range(2):
        z = w0_ref[d:d + 1, :] + _dot3(tw, wup_ref[d])
        lw_out[d, 0] = -math.exp(-0.5) * _sigmoid(z)
        a = _sigmoid(a0_ref[d:d + 1, :] + _dot3(ad, aup_ref[d]))
        a_out[d, 0] = a
        kd = k * (1.0 + (a - 1.0) * ka_ref[...])
        rk_sum = rk_sum + _dot_exact_lhs(r * kd * rk_ref[...], ones)
    r_out[0] = r
    k_out[0] = k
    v_out[0] = v
    kkn_out[0] = kkn
    g_out[0] = _dot(_sigmoid(gd), gup_ref[...])
    bonus_out[0] = rk_sum * v


def rwkv_prep(proj, conv, w0, w_up, a0, a_up, g_up, k_k, k_a, r_k, tt):
    b, t, _ = proj.shape
    hw = RWKV_WIDTH
    nblk8 = t // 8
    cb = 0
    full = lambda shape: pl.BlockSpec(shape, lambda bi, i: (0,) * len(shape))
    tok = pl.BlockSpec((1, tt, hw), lambda bi, i: (bi, i, 0))
    tok2 = pl.BlockSpec((2, 1, tt, hw), lambda bi, i: (0, bi, i, 0))
    sds = jax.ShapeDtypeStruct((b, t, hw), f32)
    sds2 = jax.ShapeDtypeStruct((2, b, t, hw), f32)
    return pl.pallas_call(
        functools.partial(_rwkv_prep_kernel, tt=tt),
        grid=(b, t // tt),
        in_specs=[pl.BlockSpec((1, tt, RWKV_IN), lambda bi, i: (bi, i, cb)),
                  pl.BlockSpec((1, 8, RWKV_IN), lambda bi, i: (bi, jnp.maximum(i * (tt // 8) - 1, 0), cb)),
                  pl.BlockSpec((1, 8, RWKV_IN), lambda bi, i: (bi, jnp.minimum((i + 1) * (tt // 8), nblk8 - 1), cb)),
                  full((3, RWKV_IN)), full((2, hw)), full((2, W_LORA, hw)), full((2, hw)),
                  full((2, A_LORA, hw)), full((G_LORA, hw)), full((1, hw)), full((1, hw)), full((1, hw))],
        out_specs=[tok, tok, tok, tok, tok, tok, tok2, tok2],
        out_shape=[sds, sds, sds, sds, sds, sds, sds2, sds2],
        compiler_params=_params("parallel", "arbitrary"),
        name="rwkv_prep",
    )(proj, proj, proj, conv, w0, w_up, a0, a_up, g_up,
      k_k.reshape(1, hw), k_a.reshape(1, hw), r_k.reshape(1, hw))


def _head_masks():
    lane_head = lax.broadcasted_iota(jnp.int32, (CHUNK, WIDE), 1) // RWKV_HEAD
    return [jnp.where(lane_head == h, 1.0, 0.0).astype(bf16) for h in range(RWKV_HEADS)]


def _stack(x, masks):
    xb = x.astype(bf16)
    return jnp.concatenate([xb * mk for mk in masks], axis=0)


def _mm(a, b, masks):
    return _dot(a, _stack(b, masks))


def _unit_triangular_inverse(ms, eye, masks):
    r = lax.broadcasted_iota(jnp.int32, eye.shape, 0) // INV_BLOCK
    c = (lax.broadcasted_iota(jnp.int32, eye.shape, 1) % CHUNK) // INV_BLOCK
    mm = lambda a, b: _mm(a, b, masks)
    d = [jnp.where(r == c, m, 0.0) for m in ms]
    off = _each(lambda m, dd: m - dd, ms, d)
    x = [-dd for dd in d]
    x2 = _each(lambda xx: mm(xx, xx), x)
    x4 = _each(lambda xx: mm(xx, xx), x2)
    p1 = _each(lambda u, w: mm(eye + u, eye + w), x, x2)
    td = _each(lambda u, w: mm(u, eye + w), p1, x4)
    z = _each(mm, td, off)
    z2 = _each(lambda zz: mm(zz, zz), z)
    z4 = _each(lambda zz: mm(zz, zz), z2)
    p3 = _each(lambda u, w: mm(eye - u, eye + w), z, z2)
    p4 = _each(lambda w, t: mm(eye + w, t), z4, td)
    return _each(mm, p3, p4)


def _rwkv_chunks(revs, lws, a_s, ks, kkns, vs, rs, k_a, h_refs):
    masks = _head_masks()
    ci = lax.broadcasted_iota(jnp.int32, (CHUNK, CHUNK), 0)
    cj = lax.broadcasted_iota(jnp.int32, (CHUNK, CHUNK), 1)
    tri_f = jnp.where(cj <= ci, 1.0, 0.0).astype(bf16)
    tri_r = jnp.where(cj >= ci, 1.0, 0.0).astype(bf16)
    ti = lax.broadcasted_iota(jnp.int32, (CHUNK, WIDE), 0)
    tj = lax.broadcasted_iota(jnp.int32, (CHUNK, WIDE), 1) % CHUNK
    eye = jnp.where(ti == tj, 1.0, 0.0)
    strict = [(tj > ti) if rev else (tj < ti) for rev in revs]
    incl = [(tj >= ti) if rev else (tj <= ti) for rev in revs]

    g_incl = [_dot_exact_rhs(tri_r if rev else tri_f, lw) for rev, lw in zip(revs, lws)]
    g_tot = [g[0:1, :] if rev else g[CHUNK - 1:CHUNK, :] for rev, g in zip(revs, g_incl)]
    kat = _each(lambda kkn, a: kkn * a, kkns, a_s)
    kd = _each(lambda k, a: k * (1.0 + (a - 1.0) * k_a), ks, a_s)
    e_neg = [jnp.exp(-g) for g in g_incl]
    e_rem = _each(lambda gt, g: jnp.exp(gt - g), g_tot, g_incl)
    a_t = _each(lambda kkn, g, lw: kkn * jnp.exp(g - lw), kkns, g_incl, lws)
    r_t = _each(lambda r, g: r * jnp.exp(g), rs, g_incl)
    b_t = _each(lambda x, e: x * e, kat, e_neg)
    k_t = _each(lambda x, e: x * e, kd, e_neg)
    b_hat = _each(lambda x, e: x * e, kat, e_rem)
    k_hat = _each(lambda x, e: x * e, kd, e_rem)

    ar = _each(lambda x, y: jnp.concatenate([x, y], axis=0).astype(bf16), a_t, r_t)
    s_b = _each(lambda x, y: _dot_nt(x, _stack(y, masks)), ar, b_t)
    s_k = _each(lambda x, y: _dot_nt(x, _stack(y, masks)), ar, k_t)
    m_ab = _each(lambda s, x: jnp.where(s, x[:CHUNK], 0.0), strict, s_b)
    a_rb = _each(lambda s, x: jnp.where(s, x[CHUNK:], 0.0), incl, s_b)
    m_ak = _each(lambda s, x: jnp.where(s, x[:CHUNK], 0.0), strict, s_k)
    a_rk = _each(lambda s, x: jnp.where(s, x[CHUNK:], 0.0), incl, s_k)
    t_inv = _unit_triangular_inverse(m_ab, eye, masks)

    wi = lax.broadcasted_iota(jnp.int32, (WIDE, WIDE), 0)
    wj = lax.broadcasted_iota(jnp.int32, (WIDE, WIDE), 1)
    same_head = (wi // RWKV_HEAD) == (wj // RWKV_HEAD)
    diag = wi == wj
    h = [h_ref[...].astype(bf16) for h_ref in h_refs]
    decay = [jnp.where(diag, jnp.exp(gt), 0.0).astype(bf16) for gt in g_tot]
    arh = _each(lambda x, dd, hh: _dot(jnp.concatenate([x, dd], axis=0), hh), ar, decay, h)
    v_s = _each(lambda x: _stack(x, masks), vs)
    mkv = _each(lambda m, n, vv: _dot(jnp.concatenate([m, n], axis=0), vv), m_ak, a_rk, v_s)
    u = _each(lambda t, x, y: -_mm(t, x[:CHUNK] + y[:CHUNK], masks), t_inv, arh, mkv)
    ys = _each(lambda x, m, uu, y: x[CHUNK:2 * CHUNK] + _mm(m, uu, masks) + y[CHUNK:], arh, a_rb, u, mkv)
    bk_t = _each(lambda x, y: jnp.concatenate([x, y], axis=0).T, b_hat, k_hat)
    for h_ref, x, bk, uu, vv in zip(h_refs, arh, bk_t, u, vs):
        upd = _dot(bk, jnp.concatenate([uu, vv], axis=0))
        h_ref[...] = x[2 * CHUNK:] + jnp.where(same_head, upd, 0.0)
    return ys


def _rwkv_scan_kernel(ka_ref,
                      lw0, a0, k0, kk0, v0, r0,
                      lw1, a1, k1, kk1, v1, r1,
                      y0_ref, y1_ref, h_ref, *, rows):
    @pl.when(pl.program_id(1) == 0)
    def _():
        h_ref[...] = jnp.zeros_like(h_ref)

    revs = [False] * rows + [True] * rows
    gather = lambda f, b: [f[0, g] for g in range(rows)] + [b[0, g] for g in range(rows)]
    plain = lambda f, b: [f[g] for g in range(rows)] + [b[g] for g in range(rows)]
    ys = _rwkv_chunks(revs, gather(lw0, lw1), gather(a0, a1), plain(k0, k1), plain(kk0, kk1),
                      plain(v0, v1), plain(r0, r1), ka_ref[...], [h_ref.at[i] for i in range(2 * rows)])
    for g in range(rows):
        y0_ref[g] = ys[g]
        y1_ref[g] = ys[rows + g]


def rwkv_scan(lw, a, k, kkn, v, r, k_a, rows):
    b, t, hw = k.shape
    nc = t // CHUNK
    fwd = pl.BlockSpec((rows, CHUNK, hw), lambda bi, c: (bi, c, 0))
    bwd = pl.BlockSpec((rows, CHUNK, hw), lambda bi, c: (bi, nc - 1 - c, 0))
    fwd2 = pl.BlockSpec((1, rows, CHUNK, hw), lambda bi, c: (0, bi, c, 0))
    bwd2 = pl.BlockSpec((1, rows, CHUNK, hw), lambda bi, c: (1, bi, nc - 1 - c, 0))
    sds = jax.ShapeDtypeStruct((b, t, hw), f32)
    return pl.pallas_call(
        functools.partial(_rwkv_scan_kernel, rows=rows),
        grid=(b // rows, nc),
        in_specs=[pl.BlockSpec((1, hw), lambda bi, c: (0, 0)),
                  fwd2, fwd2, fwd, fwd, fwd, fwd,
                  bwd2, bwd2, bwd, bwd, bwd, bwd],
        out_specs=[fwd, bwd],
        out_shape=[sds, sds],
        scratch_shapes=[pltpu.VMEM((2 * rows, WIDE, WIDE), f32)],
        compiler_params=_params("parallel", "arbitrary"),
        name="rwkv_scan",
    )(k_a.reshape(1, hw), lw, a, k, kkn, v, r, lw, a, k, kkn, v, r)


def _rwkv_finish(y0, y1, g, bonus, ln_g, ln_b):
    ones = _head_ones()
    y = y0 + y1
    mu = _dot_exact_lhs(y, ones) * (1.0 / RWKV_HEAD)
    yc = y - mu
    var = _dot_exact_lhs(yc * yc, ones) * (1.0 / RWKV_HEAD)
    yn = yc * lax.rsqrt(var + GN_EPS) * ln_g + ln_b
    return (yn + bonus) * g


def _out_proj_kernel(x_ref, a_ref, b_ref, y0_ref, y1_ref, g_ref, bonus_ref, lng_ref, lnb_ref,
                     wa_ref, wb_ref, wc_ref, o_ref):
    c = _rwkv_finish(y0_ref[...], y1_ref[...], g_ref[...], bonus_ref[...], lng_ref[...], lnb_ref[...])
    acc = jnp.dot(a_ref[...], wa_ref[...], preferred_element_type=f32)
    acc = acc + jnp.dot(b_ref[...], wb_ref[...], preferred_element_type=f32)
    acc = acc + jnp.dot(c.astype(bf16), wc_ref[...], preferred_element_type=f32)
    o_ref[...] = x_ref[...] + acc


def out_proj(x, o_win, o_diff, y0, y1, g, bonus, ln_g, ln_b, w_out, tm):
    m, d = x.shape
    hw = RWKV_WIDTH
    wa = w_out[:WIN_WIDTH]
    wb = w_out[WIN_WIDTH:WIN_WIDTH + DIFF_WIDTH]
    wc = w_out[WIN_WIDTH + DIFF_WIDTH:]
    row = lambda w: pl.BlockSpec((tm, w), lambda i: (i, 0))
    full = lambda w: pl.BlockSpec((w, d), lambda i: (0, 0))
    vec = pl.BlockSpec((1, hw), lambda i: (0, 0))
    return pl.pallas_call(
        _out_proj_kernel,
        grid=(m // tm,),
        in_specs=[row(d), row(WIN_WIDTH), row(DIFF_WIDTH), row(hw), row(hw), row(hw), row(hw), vec, vec,
                  full(WIN_WIDTH), full(DIFF_WIDTH), full(hw)],
        out_specs=row(d),
        out_shape=jax.ShapeDtypeStruct((m, d), f32),
        compiler_params=_params("parallel"),
        name="out_proj",
    )(x, o_win, o_diff, y0, y1, g, bonus, ln_g.reshape(1, hw), ln_b.reshape(1, hw), wa, wb, wc)


def _cross_kernel(x_ref, g_ref, wq_ref, kv_ref, wo_ref, o_ref):
    x = x_ref[0]
    q = jnp.dot(_rms(x, g_ref[...]).astype(bf16), wq_ref[...], preferred_element_type=f32)
    kv = kv_ref[0]
    xw = X_HEADS * X_HEAD_DIM
    outs = []
    for h in range(X_HEADS):
        qh = q[:, h * X_HEAD_DIM:(h + 1) * X_HEAD_DIM]
        kh = kv[:, h * X_HEAD_DIM:(h + 1) * X_HEAD_DIM]
        vh = kv[:, xw + h * X_HEAD_DIM:xw + (h + 1) * X_HEAD_DIM]
        s = _dot_nt(qh, kh) * (X_HEAD_DIM ** -0.5)
        e = jnp.exp(s - jnp.max(s, axis=-1, keepdims=True))
        p = e / jnp.sum(e, axis=-1, keepdims=True)
        outs.append(_dot(p, vh))
    o = jnp.concatenate(outs, axis=-1)
    o_ref[0] = x + jnp.dot(o.astype(bf16), wo_ref[...], preferred_element_type=f32)


def cross_attention(x, kv, g, wq, wo, tm):
    b, t, d = x.shape
    nm = kv.shape[1]
    xw = X_HEADS * X_HEAD_DIM
    return pl.pallas_call(
        _cross_kernel,
        grid=(b, t // tm),
        in_specs=[pl.BlockSpec((1, tm, d), lambda bi, i: (bi, i, 0)),
                  pl.BlockSpec((1, d), lambda bi, i: (0, 0)),
                  pl.BlockSpec((d, xw), lambda bi, i: (0, 0)),
                  pl.BlockSpec((1, nm, 2 * xw), lambda bi, i: (bi, 0, 0)),
                  pl.BlockSpec((xw, d), lambda bi, i: (0, 0))],
        out_specs=pl.BlockSpec((1, tm, d), lambda bi, i: (bi, i, 0)),
        out_shape=jax.ShapeDtypeStruct((b, t, d), f32),
        compiler_params=_params("parallel", "parallel"),
        name="cross_attention",
    )(x, g.reshape(1, d), wq, kv, wo)


def _ffn_up_kernel(x_ref, g_ref, wg_ref, wu_ref, o_ref, h_ref):
    @pl.when(pl.program_id(1) == 0)
    def _():
        h_ref[...] = _rms(x_ref[...], g_ref[...]).astype(bf16)

    h = h_ref[...]
    gate = jnp.dot(h, wg_ref[...], preferred_element_type=f32)
    up = jnp.dot(h, wu_ref[...], preferred_element_type=f32)
    o_ref[...] = (gate * _sigmoid(gate) * up).astype(o_ref.dtype)


def ffn_up(x, g, w_gu, tm, tf):
    m, d = x.shape
    nf = D_FF // tf
    return pl.pallas_call(
        _ffn_up_kernel,
        grid=(m // tm, nf),
        in_specs=[pl.BlockSpec((tm, d), lambda i, j: (i, 0)),
                  pl.BlockSpec((1, d), lambda i, j: (0, 0)),
                  pl.BlockSpec((d, tf), lambda i, j: (0, j)),
                  pl.BlockSpec((d, tf), lambda i, j: (0, j + nf))],
        out_specs=pl.BlockSpec((tm, tf), lambda i, j: (i, j)),
        out_shape=jax.ShapeDtypeStruct((m, D_FF), bf16),
        scratch_shapes=[pltpu.VMEM((tm, d), bf16)],
        compiler_params=_params("parallel", "arbitrary"),
        name="ffn_up",
    )(x, g.reshape(1, d), w_gu, w_gu)


def _ffn_down_kernel(x_ref, a_ref, w_ref, o_ref):
    o_ref[...] = x_ref[...] + jnp.dot(a_ref[...], w_ref[...], preferred_element_type=f32)


def _ffn_down_norm_kernel(x_ref, a_ref, w_ref, g_ref, o_ref):
    y = x_ref[...] + jnp.dot(a_ref[...], w_ref[...], preferred_element_type=f32)
    o_ref[...] = _rms(y, g_ref[...])


def ffn_down(x, act, w_down, tm, final_gain=None):
    m, d = x.shape
    kf = act.shape[1]
    in_specs = [pl.BlockSpec((tm, d), lambda i: (i, 0)),
                pl.BlockSpec((tm, kf), lambda i: (i, 0)),
                pl.BlockSpec((kf, d), lambda i: (0, 0))]
    args = [x, act, w_down]
    if final_gain is not None:
        in_specs.append(pl.BlockSpec((1, d), lambda i: (0, 0)))
        args.append(final_gain.reshape(1, d))
    return pl.pallas_call(
        _ffn_down_kernel if final_gain is None else _ffn_down_norm_kernel,
        grid=(m // tm,),
        in_specs=in_specs,
        out_specs=pl.BlockSpec((tm, d), lambda i: (i, 0)),
        out_shape=jax.ShapeDtypeStruct((m, d), f32),
        compiler_params=_params("parallel"),
        name="ffn_down",
    )(*args)


def _row_tile(m, want):
    t = min(want, m)
    while m % t:
        t //= 2
    return t


def kernel(x, mem, norm_mix, w_in, win_sink, diff_lq1, diff_lk1, diff_lq2, diff_lk2, diff_subln,
           rwkv_conv, rwkv_w0, rwkv_w_up, rwkv_a0, rwkv_a_up, rwkv_g_up, rwkv_k_k, rwkv_k_a, rwkv_r_k,
           rwkv_ln_g, rwkv_ln_b, w_out, norm_cross, norm_mem, x_wq, x_wkv, x_wo, norm_ffn,
           ffn_w_gu, ffn_w_down, norm_final):
    b, t, d = x.shape
    nm = mem.shape[1]
    m = b * t
    tm = _row_tile(m, 1024)
    depth = w_in.shape[0]
    w_in_b, w_out_b, wq_b, wkv_b, wo_b, wgu_b, wdn_b = (
        z.astype(bf16) for z in (w_in, w_out, x_wq, x_wkv, x_wo, ffn_w_gu, ffn_w_down))
    mem2 = mem.reshape(b * nm, d)
    xf = x.reshape(m, d)
    for l in range(depth):
        qa, kvb, cin = (z.reshape(b, t, IN_GROUP) for z in in_proj(xf, norm_mix[l], w_in_b[l], tm))
        o_win = window_attention(qa, win_sink[l], _row_tile(t, 256))
        lam_init = 0.8 - 0.6 * math.exp(-0.3 * l)
        o_diff = diff_attention(qa, kvb, diff_lq1[l], diff_lk1[l], diff_lq2[l], diff_lk2[l], diff_subln[l],
                                lam_init, _row_tile(t, 256))
        r, k, v, kkn, g, bonus, lw, a = rwkv_prep(
            cin, rwkv_conv[l], rwkv_w0[l], rwkv_w_up[l], rwkv_a0[l], rwkv_a_up[l], rwkv_g_up[l],
            rwkv_k_k[l], rwkv_k_a[l], rwkv_r_k[l], _row_tile(t, 256))
        y0, y1 = rwkv_scan(lw, a, k, kkn, v, r, rwkv_k_a[l], 4 if b % 4 == 0 else 1)
        flat = lambda z: z.reshape(m, z.shape[-1])
        xf = out_proj(xf, flat(o_win), flat(o_diff), flat(y0), flat(y1), flat(g), flat(bonus),
                      rwkv_ln_g[l], rwkv_ln_b[l], w_out_b[l], tm)
        kv = norm_matmul(mem2, norm_mem, wkv_b[l], _row_tile(b * nm, 512), 1024).reshape(b, nm, -1)
        xf = cross_attention(xf.reshape(b, t, d), kv, norm_cross[l], wq_b[l], wo_b[l],
                             _row_tile(t, 256)).reshape(m, d)
        act = ffn_up(xf, norm_ffn[l], wgu_b[l], tm, D_FF // 2)
        xf = ffn_down(xf, act, wdn_b[l], tm, norm_final if l == depth - 1 else None)
    return xf.reshape(b, t, d)
```

```python
import functools
import math

import jax
import jax.numpy as jnp
from jax import lax
from jax.experimental import pallas as pl
from jax.experimental.pallas import tpu as pltpu

f32 = jnp.float32
bf16 = jnp.bfloat16

D_MODEL = 1024
DEPTH = 4
HEAD_DIM = 64
BLOCK = 128
WINDOW = 128
WIN_HEADS = 4
WIN_KV_HEADS = 2
DIFF_HEADS = 4
DIFF_QK_DIM = 64
DIFF_V_DIM = 2 * DIFF_QK_DIM
RWKV_HEADS = 4
RWKV_HEAD = 64
W_LORA = 64
A_LORA = 64
G_LORA = 128
X_HEADS = 4
X_HEAD_DIM = 128
D_FF = -(-8 * D_MODEL // (3 * 256)) * 256
RMS_EPS = 1e-6
GN_EPS = 64e-5

WIN_WIDTH = WIN_HEADS * HEAD_DIM
DIFF_WIDTH = DIFF_HEADS * DIFF_V_DIM
RWKV_WIDTH = RWKV_HEADS * RWKV_HEAD
RWKV_IN = 3 * RWKV_WIDTH + W_LORA + A_LORA + G_LORA
IN_WIDTH = 2 * WIN_WIDTH + 3 * DIFF_WIDTH + RWKV_IN

OFF_AQ = 0
OFF_AK = WIN_WIDTH
OFF_AV = OFF_AK + WIN_KV_HEADS * HEAD_DIM
OFF_BQ = OFF_AV + WIN_KV_HEADS * HEAD_DIM
OFF_BK = OFF_BQ + DIFF_WIDTH
OFF_BV = OFF_BK + DIFF_WIDTH
OFF_C = OFF_BV + DIFF_WIDTH
IN_GROUP = 1024
assert OFF_BK == IN_GROUP and OFF_C == 2 * IN_GROUP and IN_WIDTH == 3 * IN_GROUP and RWKV_IN == IN_GROUP

CHUNK = 64
INV_BLOCK = 8
assert INV_BLOCK == 8 and CHUNK == 8 * INV_BLOCK
WIDE = RWKV_HEADS * CHUNK
V7X_VMEM_LIMIT = 56 * 1024 * 1024
LOG2E = 1.4426950408889634
SOFTMAX_ROWS = 16

_SLOPES = [2.0 ** (-8.0 * i / (WIN_HEADS + DIFF_HEADS)) for i in range(1, WIN_HEADS + DIFF_HEADS + 1)]
SLOPES_WIN = _SLOPES[0::2]
SLOPES_DIFF = _SLOPES[1::2]


def _params(*sem):
    return pltpu.CompilerParams(dimension_semantics=sem, vmem_limit_bytes=V7X_VMEM_LIMIT)


def _dot(a, b):
    return jnp.dot(a.astype(bf16), b.astype(bf16), preferred_element_type=f32)


def _dot_nt(a, b):
    return lax.dot_general(a.astype(bf16), b.astype(bf16), (((1,), (1,)), ((), ())),
                           preferred_element_type=f32)


def _each(fn, *lists):
    return [fn(*args) for args in zip(*lists)]


def _split2(a):
    hi = a.astype(bf16)
    lo = (a - hi.astype(f32)).astype(bf16)
    return hi, lo


def _split3(a):
    hi = a.astype(bf16)
    r1 = a - hi.astype(f32)
    mid = r1.astype(bf16)
    lo = (r1 - mid.astype(f32)).astype(bf16)
    return hi, mid, lo


def _dot3(a, b, nt=False):
    d = _dot_nt if nt else _dot
    ah, al = _split2(a)
    bh, bl = _split2(b)
    return d(ah, bh) + (d(ah, bl) + d(al, bh))


def _dot_exact_rhs(a_bf16, b):
    b1, b2, b3 = _split3(b)
    return (jnp.dot(a_bf16, b1, preferred_element_type=f32)
            + (jnp.dot(a_bf16, b2, preferred_element_type=f32)
               + jnp.dot(a_bf16, b3, preferred_element_type=f32)))


def _group_sum(a, ones_bf16):
    a1, a2 = _split2(a)
    return (jnp.dot(a1, ones_bf16, preferred_element_type=f32)
            + jnp.dot(a2, ones_bf16, preferred_element_type=f32))


def _rms(x, g):
    ms = jnp.mean(x * x, axis=-1, keepdims=True)
    return (x * lax.rsqrt(ms + RMS_EPS)) * g


def _sigmoid(z):
    return 1.0 / (1.0 + jnp.exp(-z))


def _norm_matmul_kernel(x_ref, g_ref, w_ref, o_ref, h_ref):
    @pl.when(pl.program_id(1) == 0)
    def _():
        h_ref[...] = _rms(x_ref[...], g_ref[...]).astype(bf16)

    o_ref[...] = jnp.dot(h_ref[...], w_ref[...], preferred_element_type=f32).astype(o_ref.dtype)


def norm_matmul(x, g, w, tm, tn, out_dtype=f32):
    m, k = x.shape
    n = w.shape[1]
    return pl.pallas_call(
        _norm_matmul_kernel,
        grid=(m // tm, n // tn),
        in_specs=[pl.BlockSpec((tm, k), lambda i, j: (i, 0)),
                  pl.BlockSpec((1, k), lambda i, j: (0, 0)),
                  pl.BlockSpec((k, tn), lambda i, j: (0, j))],
        out_specs=pl.BlockSpec((tm, tn), lambda i, j: (i, j)),
        out_shape=jax.ShapeDtypeStruct((m, n), out_dtype),
        scratch_shapes=[pltpu.VMEM((tm, k), bf16)],
        compiler_params=_params("parallel", "arbitrary"),
        name="norm_matmul",
    )(x, g.reshape(1, k), w)


def _in_proj_kernel(x_ref, g_ref, w_ref, oa_ref, ob_ref, oc_ref, h_ref):
    j = pl.program_id(1)

    @pl.when(j == 0)
    def _():
        h_ref[...] = _rms(x_ref[...], g_ref[...]).astype(bf16)

    for idx, ref in enumerate((oa_ref, ob_ref, oc_ref)):
        @pl.when(j == idx)
        def _(ref=ref):
            ref[...] = jnp.dot(h_ref[...], w_ref[...], preferred_element_type=f32).astype(ref.dtype)


def in_proj(x, g, w, tm):
    m, k = x.shape
    assert w.shape[1] == 3 * IN_GROUP
    out = pl.BlockSpec((tm, IN_GROUP), lambda i, j: (i, 0))
    return pl.pallas_call(
        _in_proj_kernel,
        grid=(m // tm, 3),
        in_specs=[pl.BlockSpec((tm, k), lambda i, j: (i, 0)),
                  pl.BlockSpec((1, k), lambda i, j: (0, 0)),
                  pl.BlockSpec((k, IN_GROUP), lambda i, j: (0, j))],
        out_specs=[out, out, out],
        out_shape=[jax.ShapeDtypeStruct((m, IN_GROUP), f32), jax.ShapeDtypeStruct((m, IN_GROUP), bf16),
                   jax.ShapeDtypeStruct((m, IN_GROUP), f32)],
        scratch_shapes=[pltpu.VMEM((tm, k), bf16)],
        compiler_params=_params("parallel", "arbitrary"),
        name="in_proj",
    )(x, g.reshape(1, k), w)


def _win_kernel(sink_ref, q_ref, k_ref, v_ref, o_ref, *, seq, tq):
    i = pl.program_id(1)
    span = tq + 2 * BLOCK
    start = pl.multiple_of(jnp.clip(i * tq - BLOCK, 0, seq - span), BLOCK)
    kw = k_ref[0, pl.ds(start, span), :].astype(bf16)
    vw = v_ref[0, pl.ds(start, span), :].astype(bf16)
    q = (q_ref[0] * (HEAD_DIM ** -0.5 * LOG2E)).astype(bf16)
    qpos = i * tq + lax.broadcasted_iota(jnp.int32, (tq, span), 0)
    kpos = start + lax.broadcasted_iota(jnp.int32, (tq, span), 1)
    dist = jnp.abs(kpos - qpos)
    valid = dist <= WINDOW
    distf = dist.astype(f32)
    group = WIN_HEADS // WIN_KV_HEADS
    heads = list(range(WIN_HEADS))
    kv_of = lambda x, head: x[:, (head // group) * HEAD_DIM:(head // group + 1) * HEAD_DIM]
    sinks = [sink_ref[head] * LOG2E for head in heads]
    s = [_dot_nt(q[:, head * HEAD_DIM:(head + 1) * HEAD_DIM], kv_of(kw, head)) for head in heads]
    s = [jnp.where(valid, x - (SLOPES_WIN[head] * LOG2E) * distf, -jnp.inf) for head, x in zip(heads, s)]
    m = _each(lambda x, sk: jnp.maximum(jnp.max(x, axis=-1, keepdims=True), sk), s, sinks)
    e = _each(lambda x, mm: jnp.exp2(x - mm), s, m)
    denom = _each(lambda x, mm, sk: jnp.sum(x, axis=-1, keepdims=True) + jnp.exp2(sk - mm), e, m, sinks)
    pv = [jnp.dot(x.astype(bf16), kv_of(vw, head), preferred_element_type=f32) for head, x in zip(heads, e)]
    outs = _each(lambda x, dd: x / dd, pv, denom)
    o_ref[0] = jnp.concatenate(outs, axis=-1).astype(o_ref.dtype)


def window_attention(proj, sink, tq):
    b, t, _ = proj.shape
    kvw = WIN_KV_HEADS * HEAD_DIM
    return pl.pallas_call(
        functools.partial(_win_kernel, seq=t, tq=tq),
        grid=(b, t // tq),
        in_specs=[pl.BlockSpec(memory_space=pltpu.SMEM),
                  pl.BlockSpec((1, tq, WIN_WIDTH), lambda bi, i: (bi, i, OFF_AQ // WIN_WIDTH)),
                  pl.BlockSpec((1, t, kvw), lambda bi, i: (bi, 0, OFF_AK // kvw)),
                  pl.BlockSpec((1, t, kvw), lambda bi, i: (bi, 0, OFF_AV // kvw))],
        out_specs=pl.BlockSpec((1, tq, WIN_WIDTH), lambda bi, i: (bi, i, 0)),
        out_shape=jax.ShapeDtypeStruct((b, t, WIN_WIDTH), bf16),
        compiler_params=_params("parallel", "arbitrary"),
        name="window_attention",
    )(sink, proj, proj, proj)


def _diff_kernel(slope_ref, lq1_ref, lk1_ref, lq2_ref, lk2_ref, g_ref, dist_ref, q_ref, k_ref, v_ref, o_ref,
                 s_ref, e_ref, vt_ref, mx_ref, *, seq, tq, lam_init):
    h = pl.program_id(1)
    i = pl.program_id(2)
    nq = seq // tq
    nkb = seq // BLOCK
    groups = BLOCK // 8
    kt = 2 * BLOCK

    @pl.when(i == 0)
    def _():
        r = lax.broadcasted_iota(jnp.int32, (DIFF_V_DIM, DIFF_V_DIM), 0)
        c = lax.broadcasted_iota(jnp.int32, (DIFF_V_DIM, DIFF_V_DIM), 1)
        eye = jnp.where(r == c, 1.0, 0.0).astype(bf16)
        vt_ref[...] = _dot_nt(eye, v_ref[0]).astype(bf16)

    def step(do_a, do_b, slot_a):
        slot_b = 1 - slot_a
        if do_a:
            q = q_ref[0] * (DIFF_QK_DIM ** -0.5 * LOG2E)
            lane = lax.broadcasted_iota(jnp.int32, q.shape, 1)
            qm = [jnp.where((lane // DIFF_QK_DIM) == c, q, 0.0).astype(bf16) for c in range(2)]
            first = (nq - 1 - i) * (tq // BLOCK)
            slope = slope_ref[h] * LOG2E
            mx = [None, None]
        if do_b:
            m = [jnp.max(mx_ref[slot_b, c], axis=0, keepdims=True) for c in range(2)]
            sm = [None, None]
            acc = [None, None]
        for kb in range(nkb):
            rows = slice(kb * BLOCK, (kb + 1) * BLOCK)
            if do_a:
                kblk = k_ref[0, rows, :]
                bias = slope * dist_ref[first + kb]
                for c in range(2):
                    s = _dot_nt(kblk, qm[c]) - bias
                    s_ref[slot_a, c, rows, :] = s
                    part = jnp.max(s.reshape(groups, 8, tq), axis=0)
                    mx[c] = part if mx[c] is None else jnp.maximum(mx[c], part)
            if do_b:
                for c in range(2):
                    e = jnp.exp2(s_ref[slot_b, c, rows, :] - m[c])
                    e_ref[c, rows, :] = e.astype(bf16)
                    part = jnp.sum(e.reshape(groups, 8, tq), axis=0)
                    sm[c] = part if sm[c] is None else sm[c] + part
                if (kb + 1) % (kt // BLOCK) == 0:
                    keys = slice((kb + 1) * BLOCK - kt, (kb + 1) * BLOCK)
                    for c in range(2):
                        part = jnp.dot(vt_ref[:, keys], e_ref[c, keys, :], preferred_element_type=f32)
                        acc[c] = part if acc[c] is None else acc[c] + part
        if do_a:
            for c in range(2):
                mx_ref[slot_a, c] = mx[c]
        if do_b:
            lam = (jnp.exp(jnp.sum(lq1_ref[...] * lk1_ref[...], axis=-1, keepdims=True))
                   - jnp.exp(jnp.sum(lq2_ref[...] * lk2_ref[...], axis=-1, keepdims=True)) + lam_init)
            outs = [acc[c] / jnp.sum(sm[c], axis=0, keepdims=True) for c in range(2)]
            ot = outs[0] - lam * outs[1]
            ms = jnp.mean(ot * ot, axis=0, keepdims=True)
            ot = ot * lax.rsqrt(ms + RMS_EPS) * g_ref[...] * (1.0 - lam_init)
            o_ref[0] = ot.T.astype(o_ref.dtype)

    middle = (i > 0) & (i < nq)
    pl.when(i == 0)(lambda: step(True, False, 0))
    pl.when(middle & (i % 2 == 0))(lambda: step(True, True, 0))
    pl.when(middle & (i % 2 == 1))(lambda: step(True, True, 1))
    pl.when(i == nq)(lambda: step(False, True, nq % 2))


def _distance_table(seq, tq):
    off = (seq // tq - 1) * (tq // BLOCK)
    nj = seq // BLOCK + off
    j = lax.broadcasted_iota(jnp.int32, (nj, BLOCK, tq), 0)
    kl = lax.broadcasted_iota(jnp.int32, (nj, BLOCK, tq), 1)
    q = lax.broadcasted_iota(jnp.int32, (nj, BLOCK, tq), 2)
    return jnp.abs((j - off) * BLOCK + kl - q).astype(f32)


def diff_attention(qa, kv, lq1, lk1, lq2, lk2, subln, lam_init, tq):
    b, t, _ = qa.shape
    w = DIFF_V_DIM
    vec = pl.BlockSpec((1, DIFF_QK_DIM), lambda bi, h, i: (0, 0))
    slopes = jnp.asarray(SLOPES_DIFF, f32)
    dist = _distance_table(t, tq)
    nq = t // tq
    return pl.pallas_call(
        functools.partial(_diff_kernel, seq=t, tq=tq, lam_init=lam_init),
        grid=(b, DIFF_HEADS, nq + 1),
        in_specs=[pl.BlockSpec(memory_space=pltpu.SMEM), vec, vec, vec, vec,
                  pl.BlockSpec((w, 1), lambda bi, h, i: (0, 0)),
                  pl.BlockSpec(dist.shape, lambda bi, h, i: (0, 0, 0)),
                  pl.BlockSpec((1, tq, w), lambda bi, h, i: (bi, jnp.minimum(i, nq - 1), OFF_BQ // w + h)),
                  pl.BlockSpec((1, t, w), lambda bi, h, i: (bi, 0, (OFF_BK - IN_GROUP) // w + h)),
                  pl.BlockSpec((1, t, w), lambda bi, h, i: (bi, 0, (OFF_BV - IN_GROUP) // w + h))],
        out_specs=pl.BlockSpec((1, tq, w), lambda bi, h, i: (bi, jnp.maximum(i - 1, 0), h)),
        out_shape=jax.ShapeDtypeStruct((b, t, DIFF_WIDTH), bf16),
        scratch_shapes=[pltpu.VMEM((2, 2, t, tq), f32), pltpu.VMEM((2, t, tq), bf16), pltpu.VMEM((w, t), bf16),
                        pltpu.VMEM((2, 2, 8, tq), f32)],
        compiler_params=_params("parallel", "parallel", "arbitrary"),
        name="diff_attention",
    )(slopes, lq1.reshape(1, -1), lk1.reshape(1, -1), lq2.reshape(1, -1), lk2.reshape(1, -1),
      subln.reshape(-1, 1), dist, qa, kv, kv)


def _head_ones():
    r = lax.broadcasted_iota(jnp.int32, (RWKV_WIDTH, RWKV_WIDTH), 0) // RWKV_HEAD
    c = lax.broadcasted_iota(jnp.int32, (RWKV_WIDTH, RWKV_WIDTH), 1) // RWKV_HEAD
    return jnp.where(r == c, 1.0, 0.0).astype(bf16)


def _rwkv_prep_kernel(x_ref, xp_ref, xn_ref, conv_ref, w0_ref, wup_ref, a0_ref, aup_ref, gup_ref,
                      kk_ref, ka_ref, rk_ref,
                      r_out, k_out, v_out, kkn_out, g_out, bonus_out, lw_out, a_out, *, tt):
    i = pl.program_id(1)
    n = pl.num_programs(1)
    x = x_ref[0]
    first = jnp.where(i > 0, xp_ref[0, 7:8, :], 0.0)
    last = jnp.where(i < n - 1, xn_ref[0, 0:1, :], 0.0)
    c = pltpu.roll(x, 1, 0) * conv_ref[0:1, :] + x * conv_ref[1:2, :] + pltpu.roll(x, tt - 1, 0) * conv_ref[2:3, :]
    row8 = lax.broadcasted_iota(jnp.int32, (8, 1), 0)
    head = c[0:8] + jnp.where(row8 == 0, (first - x[tt - 1:tt]) * conv_ref[0:1, :], 0.0)
    tail = c[tt - 8:tt] + jnp.where(row8 == 7, (last - x[0:1]) * conv_ref[2:3, :], 0.0)
    c = jnp.concatenate([head, c[8:tt - 8], tail], axis=0)
    hw = RWKV_WIDTH
    r = c[:, 0:hw]
    k = c[:, hw:2 * hw]
    v = c[:, 2 * hw:3 * hw]
    o = 3 * hw
    wd = c[:, o:o + W_LORA]
    ad = c[:, o + W_LORA:o + W_LORA + A_LORA]
    gd = c[:, o + W_LORA + A_LORA:o + W_LORA + A_LORA + G_LORA]
    ones = _head_ones()
    kscaled = k * kk_ref[...]
    norm = jnp.sqrt(_group_sum(kscaled * kscaled, ones))
    kkn = kscaled / jnp.maximum(norm, 1e-12)
    tw = jnp.tanh(wd)
    rk_sum = None
    for d in range(2):
        z = w0_ref[d:d + 1, :] + _dot3(tw, wup_ref[d])
        lw_out[d, 0] = -math.exp(-0.5) * _sigmoid(z)
        a = _sigmoid(a0_ref[d:d + 1, :] + _dot(ad, aup_ref[d]))
        a_out[d, 0] = a
        kd = k * (1.0 + (a - 1.0) * ka_ref[...])
        rk = r * kd * rk_ref[...]
        rk_sum = rk if d == 0 else rk_sum + rk
    r_out[0] = r
    k_out[0] = k
    v_out[0] = v
    kkn_out[0] = kkn
    g_out[0] = _dot(_sigmoid(gd), gup_ref[...])
    bonus_out[0] = _group_sum(rk_sum, ones) * v


def rwkv_prep(proj, conv, w0, w_up, a0, a_up, g_up, k_k, k_a, r_k, tt):
    b, t, _ = proj.shape
    hw = RWKV_WIDTH
    nblk8 = t // 8
    cb = 0
    full = lambda shape: pl.BlockSpec(shape, lambda bi, i: (0,) * len(shape))
    tok = pl.BlockSpec((1, tt, hw), lambda bi, i: (bi, i, 0))
    tok2 = pl.BlockSpec((2, 1, tt, hw), lambda bi, i: (0, bi, i, 0))
    sds = jax.ShapeDtypeStruct((b, t, hw), f32)
    sds2 = jax.ShapeDtypeStruct((2, b, t, hw), f32)
    return pl.pallas_call(
        functools.partial(_rwkv_prep_kernel, tt=tt),
        grid=(b, t // tt),
        in_specs=[pl.BlockSpec((1, tt, RWKV_IN), lambda bi, i: (bi, i, cb)),
                  pl.BlockSpec((1, 8, RWKV_IN), lambda bi, i: (bi, jnp.maximum(i * (tt // 8) - 1, 0), cb)),
                  pl.BlockSpec((1, 8, RWKV_IN), lambda bi, i: (bi, jnp.minimum((i + 1) * (tt // 8), nblk8 - 1), cb)),
                  full((3, RWKV_IN)), full((2, hw)), full((2, W_LORA, hw)), full((2, hw)),
                  full((2, A_LORA, hw)), full((G_LORA, hw)), full((1, hw)), full((1, hw)), full((1, hw))],
        out_specs=[tok, tok, tok, tok, tok, tok, tok2, tok2],
        out_shape=[sds, sds, sds, sds, sds, sds, sds2, sds2],
        compiler_params=_params("parallel", "arbitrary"),
        name="rwkv_prep",
    )(proj, proj, proj, conv, w0, w_up, a0, a_up, g_up,
      k_k.reshape(1, hw), k_a.reshape(1, hw), r_k.reshape(1, hw))


def _head_masks():
    lane_head = lax.broadcasted_iota(jnp.int32, (CHUNK, WIDE), 1) // RWKV_HEAD
    return [jnp.where(lane_head == h, 1.0, 0.0).astype(bf16) for h in range(RWKV_HEADS)]


def _stack(x, masks):
    xb = x.astype(bf16)
    return jnp.concatenate([xb * mk for mk in masks], axis=0)


def _mm(a, b, masks):
    return _dot(a, _stack(b, masks))


def _unit_triangular_inverse(ms, eye, masks):
    r = lax.broadcasted_iota(jnp.int32, eye.shape, 0) // INV_BLOCK
    c = (lax.broadcasted_iota(jnp.int32, eye.shape, 1) % CHUNK) // INV_BLOCK
    mm = lambda a, b: _mm(a, b, masks)
    d = [jnp.where(r == c, m, 0.0) for m in ms]
    off = _each(lambda m, dd: m - dd, ms, d)
    x = [-dd for dd in d]
    x2 = _each(lambda xx: mm(xx, xx), x)
    x4 = _each(lambda xx: mm(xx, xx), x2)
    p1 = _each(lambda u, w: mm(eye + u, eye + w), x, x2)
    td = _each(lambda u, w: mm(u, eye + w), p1, x4)
    z = _each(mm, td, off)
    z2 = _each(lambda zz: mm(zz, zz), z)
    z4 = _each(lambda zz: mm(zz, zz), z2)
    p3 = _each(lambda u, w: mm(eye - u, eye + w), z, z2)
    p4 = _each(lambda w, t: mm(eye + w, t), z4, td)
    return _each(mm, p3, p4)


def _rwkv_chunks(revs, lws, a_s, ks, kkns, vs, rs, k_a, h_refs):
    masks = _head_masks()
    ci = lax.broadcasted_iota(jnp.int32, (CHUNK, CHUNK), 0)
    cj = lax.broadcasted_iota(jnp.int32, (CHUNK, CHUNK), 1)
    tri_f = jnp.where(cj <= ci, 1.0, 0.0).astype(bf16)
    tri_r = jnp.where(cj >= ci, 1.0, 0.0).astype(bf16)
    ti = lax.broadcasted_iota(jnp.int32, (CHUNK, WIDE), 0)
    tj = lax.broadcasted_iota(jnp.int32, (CHUNK, WIDE), 1) % CHUNK
    eye = jnp.where(ti == tj, 1.0, 0.0)
    strict = [(tj > ti) if rev else (tj < ti) for rev in revs]
    incl = [(tj >= ti) if rev else (tj <= ti) for rev in revs]

    g_incl = [_dot_exact_rhs(tri_r if rev else tri_f, lw) for rev, lw in zip(revs, lws)]
    g_tot = [g[0:1, :] if rev else g[CHUNK - 1:CHUNK, :] for rev, g in zip(revs, g_incl)]
    kat = _each(lambda kkn, a: kkn * a, kkns, a_s)
    kd = _each(lambda k, a: k * (1.0 + (a - 1.0) * k_a), ks, a_s)
    e_neg = [jnp.exp(-g) for g in g_incl]
    e_rem = _each(lambda gt, g: jnp.exp(gt - g), g_tot, g_incl)
    a_t = _each(lambda kkn, g, lw: kkn * jnp.exp(g - lw), kkns, g_incl, lws)
    r_t = _each(lambda r, g: r * jnp.exp(g), rs, g_incl)
    b_t = _each(lambda x, e: x * e, kat, e_neg)
    k_t = _each(lambda x, e: x * e, kd, e_neg)
    b_hat = _each(lambda x, e: x * e, kat, e_rem)
    k_hat = _each(lambda x, e: x * e, kd, e_rem)

    ar = _each(lambda x, y: jnp.concatenate([x, y], axis=0).astype(bf16), a_t, r_t)
    s_b = _each(lambda x, y: _dot_nt(x, _stack(y, masks)), ar, b_t)
    s_k = _each(lambda x, y: _dot_nt(x, _stack(y, masks)), ar, k_t)
    m_ab = _each(lambda s, x: jnp.where(s, x[:CHUNK], 0.0), strict, s_b)
    a_rb = _each(lambda s, x: jnp.where(s, x[CHUNK:], 0.0), incl, s_b)
    m_ak = _each(lambda s, x: jnp.where(s, x[:CHUNK], 0.0), strict, s_k)
    a_rk = _each(lambda s, x: jnp.where(s, x[CHUNK:], 0.0), incl, s_k)
    t_inv = _unit_triangular_inverse(m_ab, eye, masks)

    wi = lax.broadcasted_iota(jnp.int32, (WIDE, WIDE), 0)
    wj = lax.broadcasted_iota(jnp.int32, (WIDE, WIDE), 1)
    same_head = (wi // RWKV_HEAD) == (wj // RWKV_HEAD)
    diag = wi == wj
    h = [h_ref[...].astype(bf16) for h_ref in h_refs]
    decay = [jnp.where(diag, jnp.exp(gt), 0.0).astype(bf16) for gt in g_tot]
    arh = _each(lambda x, dd, hh: _dot(jnp.concatenate([x, dd], axis=0), hh), ar, decay, h)
    v_s = _each(lambda x: _stack(x, masks), vs)
    mkv = _each(lambda m, n, vv: _dot(jnp.concatenate([m, n], axis=0), vv), m_ak, a_rk, v_s)
    u = _each(lambda t, x, y: -_mm(t, x[:CHUNK] + y[:CHUNK], masks), t_inv, arh, mkv)
    ys = _each(lambda x, m, uu, y: x[CHUNK:2 * CHUNK] + _mm(m, uu, masks) + y[CHUNK:], arh, a_rb, u, mkv)
    bk_t = _each(lambda x, y: jnp.concatenate([x, y], axis=0).T, b_hat, k_hat)
    for h_ref, x, bk, uu, vv in zip(h_refs, arh, bk_t, u, vs):
        upd = _dot(bk, jnp.concatenate([uu, vv], axis=0))
        h_ref[...] = x[2 * CHUNK:] + jnp.where(same_head, upd, 0.0)
    return ys


def _rwkv_scan_kernel(ka_ref,
                      lw0, a0, k0, kk0, v0, r0,
                      lw1, a1, k1, kk1, v1, r1,
                      y0_ref, y1_ref, h_ref, *, rows):
    @pl.when(pl.program_id(1) == 0)
    def _():
        h_ref[...] = jnp.zeros_like(h_ref)

    revs = [False] * rows + [True] * rows
    gather = lambda f, b: [f[0, g] for g in range(rows)] + [b[0, g] for g in range(rows)]
    plain = lambda f, b: [f[g] for g in range(rows)] + [b[g] for g in range(rows)]
    ys = _rwkv_chunks(revs, gather(lw0, lw1), gather(a0, a1), plain(k0, k1), plain(kk0, kk1),
                      plain(v0, v1), plain(r0, r1), ka_ref[...], [h_ref.at[i] for i in range(2 * rows)])
    for g in range(rows):
        y0_ref[g] = ys[g]
        y1_ref[g] = ys[rows + g]


def rwkv_scan(lw, a, k, kkn, v, r, k_a, rows):
    b, t, hw = k.shape
    nc = t // CHUNK
    fwd = pl.BlockSpec((rows, CHUNK, hw), lambda bi, c: (bi, c, 0))
    bwd = pl.BlockSpec((rows, CHUNK, hw), lambda bi, c: (bi, nc - 1 - c, 0))
    fwd2 = pl.BlockSpec((1, rows, CHUNK, hw), lambda bi, c: (0, bi, c, 0))
    bwd2 = pl.BlockSpec((1, rows, CHUNK, hw), lambda bi, c: (1, bi, nc - 1 - c, 0))
    sds = jax.ShapeDtypeStruct((b, t, hw), f32)
    return pl.pallas_call(
        functools.partial(_rwkv_scan_kernel, rows=rows),
        grid=(b // rows, nc),
        in_specs=[pl.BlockSpec((1, hw), lambda bi, c: (0, 0)),
                  fwd2, fwd2, fwd, fwd, fwd, fwd,
                  bwd2, bwd2, bwd, bwd, bwd, bwd],
        out_specs=[fwd, bwd],
        out_shape=[sds, sds],
        scratch_shapes=[pltpu.VMEM((2 * rows, WIDE, WIDE), f32)],
        compiler_params=_params("parallel", "arbitrary"),
        name="rwkv_scan",
    )(k_a.reshape(1, hw), lw, a, k, kkn, v, r, lw, a, k, kkn, v, r)


def _rwkv_finish(y0, y1, g, bonus, ln_g, ln_b):
    ones = _head_ones()
    y = y0 + y1
    mu = _group_sum(y, ones) * (1.0 / RWKV_HEAD)
    yc = y - mu
    var = _group_sum(yc * yc, ones) * (1.0 / RWKV_HEAD)
    yn = yc * lax.rsqrt(var + GN_EPS) * ln_g + ln_b
    return (yn + bonus) * g


def _mix_cross_kernel(x_ref, a_ref, b_ref, y0_ref, y1_ref, g_ref, bonus_ref, lng_ref, lnb_ref,
                      wa_ref, wb_ref, wc_ref, gx_ref, wq_ref, kv_ref, wo_ref, o_ref):
    c = _rwkv_finish(y0_ref[0], y1_ref[0], g_ref[0], bonus_ref[0], lng_ref[...], lnb_ref[...])
    acc = jnp.dot(a_ref[0], wa_ref[...], preferred_element_type=f32)
    acc = acc + jnp.dot(b_ref[0], wb_ref[...], preferred_element_type=f32)
    acc = acc + jnp.dot(c.astype(bf16), wc_ref[...], preferred_element_type=f32)
    x = x_ref[0] + acc
    q = jnp.dot(_rms(x, gx_ref[...]).astype(bf16), wq_ref[...], preferred_element_type=f32)
    q = (q * (X_HEAD_DIM ** -0.5 * LOG2E)).astype(bf16)
    kv = kv_ref[0]
    xw = X_HEADS * X_HEAD_DIM
    heads = range(X_HEADS)
    s = [_dot_nt(q[:, h * X_HEAD_DIM:(h + 1) * X_HEAD_DIM], kv[:, h * X_HEAD_DIM:(h + 1) * X_HEAD_DIM])
         for h in heads]
    e = _each(lambda z: jnp.exp2(z - jnp.max(z, axis=-1, keepdims=True)), s)
    denom = _each(lambda z: jnp.sum(z, axis=-1, keepdims=True), e)
    pv = [jnp.dot(z.astype(bf16), kv[:, xw + h * X_HEAD_DIM:xw + (h + 1) * X_HEAD_DIM],
                  preferred_element_type=f32) for h, z in zip(heads, e)]
    o = jnp.concatenate(_each(lambda z, dd: z / dd, pv, denom), axis=-1)
    o_ref[0] = x + jnp.dot(o.astype(bf16), wo_ref[...], preferred_element_type=f32)


def mix_cross(x, o_win, o_diff, y0, y1, g, bonus, ln_g, ln_b, w_out, kv, g_cross, wq, wo, tm):
    b, t, d = x.shape
    nm = kv.shape[1]
    hw = RWKV_WIDTH
    xw = X_HEADS * X_HEAD_DIM
    wa = w_out[:WIN_WIDTH]
    wb = w_out[WIN_WIDTH:WIN_WIDTH + DIFF_WIDTH]
    wc = w_out[WIN_WIDTH + DIFF_WIDTH:]
    row = lambda w: pl.BlockSpec((1, tm, w), lambda bi, i: (bi, i, 0))
    full = lambda r, c: pl.BlockSpec((r, c), lambda bi, i: (0, 0))
    return pl.pallas_call(
        _mix_cross_kernel,
        grid=(b, t // tm),
        in_specs=[row(d), row(WIN_WIDTH), row(DIFF_WIDTH), row(hw), row(hw), row(hw), row(hw),
                  full(1, hw), full(1, hw), full(WIN_WIDTH, d), full(DIFF_WIDTH, d), full(hw, d),
                  full(1, d), full(d, xw),
                  pl.BlockSpec((1, nm, 2 * xw), lambda bi, i: (bi, 0, 0)),
                  full(xw, d)],
        out_specs=row(d),
        out_shape=jax.ShapeDtypeStruct((b, t, d), f32),
        compiler_params=_params("parallel", "parallel"),
        name="mix_cross",
    )(x, o_win, o_diff, y0, y1, g, bonus, ln_g.reshape(1, hw), ln_b.reshape(1, hw), wa, wb, wc,
      g_cross.reshape(1, d), wq, kv, wo)


def _ffn_up_kernel(x_ref, g_ref, wg_ref, wu_ref, o_ref, h_ref):
    @pl.when(pl.program_id(1) == 0)
    def _():
        h_ref[...] = _rms(x_ref[...], g_ref[...]).astype(bf16)

    h = h_ref[...]
    gate = jnp.dot(h, wg_ref[...], preferred_element_type=f32)
    up = jnp.dot(h, wu_ref[...], preferred_element_type=f32)
    o_ref[...] = (gate * _sigmoid(gate) * up).astype(o_ref.dtype)


def ffn_up(x, g, w_gu, tm, tf):
    m, d = x.shape
    nf = D_FF // tf
    return pl.pallas_call(
        _ffn_up_kernel,
        grid=(m // tm, nf),
        in_specs=[pl.BlockSpec((tm, d), lambda i, j: (i, 0)),
                  pl.BlockSpec((1, d), lambda i, j: (0, 0)),
                  pl.BlockSpec((d, tf), lambda i, j: (0, j)),
                  pl.BlockSpec((d, tf), lambda i, j: (0, j + nf))],
        out_specs=pl.BlockSpec((tm, tf), lambda i, j: (i, j)),
        out_shape=jax.ShapeDtypeStruct((m, D_FF), bf16),
        scratch_shapes=[pltpu.VMEM((tm, d), bf16)],
        compiler_params=_params("parallel", "arbitrary"),
        name="ffn_up",
    )(x, g.reshape(1, d), w_gu, w_gu)


def _ffn_down_kernel(x_ref, a_ref, w_ref, o_ref):
    o_ref[...] = x_ref[...] + jnp.dot(a_ref[...], w_ref[...], preferred_element_type=f32)


def _ffn_down_norm_kernel(x_ref, a_ref, w_ref, g_ref, o_ref):
    y = x_ref[...] + jnp.dot(a_ref[...], w_ref[...], preferred_element_type=f32)
    o_ref[...] = _rms(y, g_ref[...])


def ffn_down(x, act, w_down, tm, final_gain=None):
    m, d = x.shape
    kf = act.shape[1]
    in_specs = [pl.BlockSpec((tm, d), lambda i: (i, 0)),
                pl.BlockSpec((tm, kf), lambda i: (i, 0)),
                pl.BlockSpec((kf, d), lambda i: (0, 0))]
    args = [x, act, w_down]
    if final_gain is not None:
        in_specs.append(pl.BlockSpec((1, d), lambda i: (0, 0)))
        args.append(final_gain.reshape(1, d))
    return pl.pallas_call(
        _ffn_down_kernel if final_gain is None else _ffn_down_norm_kernel,
        grid=(m // tm,),
        in_specs=in_specs,
        out_specs=pl.BlockSpec((tm, d), lambda i: (i, 0)),
        out_shape=jax.ShapeDtypeStruct((m, d), f32),
        compiler_params=_params("parallel"),
        name="ffn_down",
    )(*args)


def _row_tile(m, want):
    t = min(want, m)
    while m % t:
        t //= 2
    return t


def kernel(x, mem, norm_mix, w_in, win_sink, diff_lq1, diff_lk1, diff_lq2, diff_lk2, diff_subln,
           rwkv_conv, rwkv_w0, rwkv_w_up, rwkv_a0, rwkv_a_up, rwkv_g_up, rwkv_k_k, rwkv_k_a, rwkv_r_k,
           rwkv_ln_g, rwkv_ln_b, w_out, norm_cross, norm_mem, x_wq, x_wkv, x_wo, norm_ffn,
           ffn_w_gu, ffn_w_down, norm_final):
    b, t, d = x.shape
    nm = mem.shape[1]
    m = b * t
    tm = _row_tile(m, 1024)
    depth = w_in.shape[0]
    w_in_b, w_out_b, wq_b, wkv_b, wo_b, wgu_b, wdn_b = (
        z.astype(bf16) for z in (w_in, w_out, x_wq, x_wkv, x_wo, ffn_w_gu, ffn_w_down))
    mem2 = mem.reshape(b * nm, d)
    xf = x.reshape(m, d)
    for l in range(depth):
        qa, kvb, cin = (z.reshape(b, t, IN_GROUP) for z in in_proj(xf, norm_mix[l], w_in_b[l], tm))
        o_win = window_attention(qa, win_sink[l], _row_tile(t, 256))
        lam_init = 0.8 - 0.6 * math.exp(-0.3 * l)
        o_diff = diff_attention(qa, kvb, diff_lq1[l], diff_lk1[l], diff_lq2[l], diff_lk2[l], diff_subln[l],
                                lam_init, _row_tile(t, 512))
        r, k, v, kkn, g, bonus, lw, a = rwkv_prep(
            cin, rwkv_conv[l], rwkv_w0[l], rwkv_w_up[l], rwkv_a0[l], rwkv_a_up[l], rwkv_g_up[l],
            rwkv_k_k[l], rwkv_k_a[l], rwkv_r_k[l], _row_tile(t, 256))
        y0, y1 = rwkv_scan(lw, a, k, kkn, v, r, rwkv_k_a[l], 4 if b % 4 == 0 else 1)
        kv = norm_matmul(mem2, norm_mem, wkv_b[l], _row_tile(b * nm, 512), 1024, bf16).reshape(b, nm, -1)
        xf = mix_cross(xf.reshape(b, t, d), o_win, o_diff, y0, y1, g, bonus, rwkv_ln_g[l], rwkv_ln_b[l],
                       w_out_b[l], kv, norm_cross[l], wq_b[l], wo_b[l], _row_tile(t, 256)).reshape(m, d)
        act = ffn_up(xf, norm_ffn[l], wgu_b[l], tm, D_FF // 2)
        xf = ffn_down(xf, act, wdn_b[l], tm, norm_final if l == depth - 1 else None)
    return xf.reshape(b, t, d)
```

```python
import functools
import math

import jax
import jax.numpy as jnp
from jax import lax
from jax.experimental import pallas as pl
from jax.experimental.pallas import tpu as pltpu

f32 = jnp.float32
bf16 = jnp.bfloat16

D_MODEL = 1024
DEPTH = 4
HEAD_DIM = 64
BLOCK = 128
WINDOW = 128
WIN_HEADS = 4
WIN_KV_HEADS = 2
DIFF_HEADS = 4
DIFF_QK_DIM = 64
DIFF_V_DIM = 2 * DIFF_QK_DIM
RWKV_HEADS = 4
RWKV_HEAD = 64
W_LORA = 64
A_LORA = 64
G_LORA = 128
X_HEADS = 4
X_HEAD_DIM = 128
D_FF = -(-8 * D_MODEL // (3 * 256)) * 256
RMS_EPS = 1e-6
GN_EPS = 64e-5

WIN_WIDTH = WIN_HEADS * HEAD_DIM
DIFF_WIDTH = DIFF_HEADS * DIFF_V_DIM
RWKV_WIDTH = RWKV_HEADS * RWKV_HEAD
RWKV_IN = 3 * RWKV_WIDTH + W_LORA + A_LORA + G_LORA
IN_WIDTH = 2 * WIN_WIDTH + 3 * DIFF_WIDTH + RWKV_IN

OFF_AQ = 0
OFF_AK = WIN_WIDTH
OFF_AV = OFF_AK + WIN_KV_HEADS * HEAD_DIM
OFF_BQ = OFF_AV + WIN_KV_HEADS * HEAD_DIM
OFF_BK = OFF_BQ + DIFF_WIDTH
OFF_BV = OFF_BK + DIFF_WIDTH
OFF_C = OFF_BV + DIFF_WIDTH
IN_GROUP = 1024
assert OFF_BK == IN_GROUP and OFF_C == 2 * IN_GROUP and IN_WIDTH == 3 * IN_GROUP and RWKV_IN == IN_GROUP

CHUNK = 64
INV_BLOCK = 8
assert INV_BLOCK == 8 and CHUNK == 8 * INV_BLOCK
WIDE = RWKV_HEADS * CHUNK
V7X_VMEM_LIMIT = 56 * 1024 * 1024
LOG2E = 1.4426950408889634
SOFTMAX_ROWS = 16

_SLOPES = [2.0 ** (-8.0 * i / (WIN_HEADS + DIFF_HEADS)) for i in range(1, WIN_HEADS + DIFF_HEADS + 1)]
SLOPES_WIN = _SLOPES[0::2]
SLOPES_DIFF = _SLOPES[1::2]


def _params(*sem):
    return pltpu.CompilerParams(dimension_semantics=sem, vmem_limit_bytes=V7X_VMEM_LIMIT)


def _dot(a, b):
    return jnp.dot(a.astype(bf16), b.astype(bf16), preferred_element_type=f32)


def _dot_nt(a, b):
    return lax.dot_general(a.astype(bf16), b.astype(bf16), (((1,), (1,)), ((), ())),
                           preferred_element_type=f32)


def _each(fn, *lists):
    return [fn(*args) for args in zip(*lists)]


def _split2(a):
    hi = a.astype(bf16)
    lo = (a - hi.astype(f32)).astype(bf16)
    return hi, lo


def _split3(a):
    hi = a.astype(bf16)
    r1 = a - hi.astype(f32)
    mid = r1.astype(bf16)
    lo = (r1 - mid.astype(f32)).astype(bf16)
    return hi, mid, lo


def _dot3(a, b, nt=False):
    d = _dot_nt if nt else _dot
    ah, al = _split2(a)
    bh, bl = _split2(b)
    return d(ah, bh) + (d(ah, bl) + d(al, bh))


def _dot_exact_rhs(a_bf16, b):
    b1, b2, b3 = _split3(b)
    return (jnp.dot(a_bf16, b1, preferred_element_type=f32)
            + (jnp.dot(a_bf16, b2, preferred_element_type=f32)
               + jnp.dot(a_bf16, b3, preferred_element_type=f32)))


def _group_sum(a, ones_bf16):
    a1, a2 = _split2(a)
    return (jnp.dot(a1, ones_bf16, preferred_element_type=f32)
            + jnp.dot(a2, ones_bf16, preferred_element_type=f32))


def _rms(x, g):
    ms = jnp.mean(x * x, axis=-1, keepdims=True)
    return (x * lax.rsqrt(ms + RMS_EPS)) * g


def _sigmoid(z):
    return 1.0 / (1.0 + jnp.exp(-z))


def _norm_matmul_kernel(x_ref, g_ref, w_ref, o_ref, h_ref):
    @pl.when(pl.program_id(1) == 0)
    def _():
        h_ref[...] = _rms(x_ref[...], g_ref[...]).astype(bf16)

    o_ref[...] = jnp.dot(h_ref[...], w_ref[...], preferred_element_type=f32).astype(o_ref.dtype)


def norm_matmul(x, g, w, tm, tn, out_dtype=f32):
    m, k = x.shape
    n = w.shape[1]
    return pl.pallas_call(
        _norm_matmul_kernel,
        grid=(m // tm, n // tn),
        in_specs=[pl.BlockSpec((tm, k), lambda i, j: (i, 0)),
                  pl.BlockSpec((1, k), lambda i, j: (0, 0)),
                  pl.BlockSpec((k, tn), lambda i, j: (0, j))],
        out_specs=pl.BlockSpec((tm, tn), lambda i, j: (i, j)),
        out_shape=jax.ShapeDtypeStruct((m, n), out_dtype),
        scratch_shapes=[pltpu.VMEM((tm, k), bf16)],
        compiler_params=_params("parallel", "arbitrary"),
        name="norm_matmul",
    )(x, g.reshape(1, k), w)


def _in_proj_kernel(x_ref, g_ref, w_ref, oa_ref, ob_ref, h_ref):
    j = pl.program_id(1)

    @pl.when(j == 0)
    def _():
        h_ref[...] = _rms(x_ref[...], g_ref[...]).astype(bf16)

    for idx, ref in enumerate((oa_ref, ob_ref)):
        @pl.when(j == idx)
        def _(ref=ref):
            ref[...] = jnp.dot(h_ref[...], w_ref[...], preferred_element_type=f32).astype(ref.dtype)


def in_proj(x, g, w, tm):
    m, k = x.shape
    assert w.shape[1] == 3 * IN_GROUP
    out = pl.BlockSpec((tm, IN_GROUP), lambda i, j: (i, 0))
    return pl.pallas_call(
        _in_proj_kernel,
        grid=(m // tm, 2),
        in_specs=[pl.BlockSpec((tm, k), lambda i, j: (i, 0)),
                  pl.BlockSpec((1, k), lambda i, j: (0, 0)),
                  pl.BlockSpec((k, IN_GROUP), lambda i, j: (0, j))],
        out_specs=[out, out],
        out_shape=[jax.ShapeDtypeStruct((m, IN_GROUP), f32), jax.ShapeDtypeStruct((m, IN_GROUP), bf16)],
        scratch_shapes=[pltpu.VMEM((tm, k), bf16)],
        compiler_params=_params("parallel", "arbitrary"),
        name="in_proj",
    )(x, g.reshape(1, k), w)


def _win_kernel(sink_ref, q_ref, k_ref, v_ref, o_ref, *, seq, tq):
    i = pl.program_id(1)
    span = tq + 2 * BLOCK
    start = pl.multiple_of(jnp.clip(i * tq - BLOCK, 0, seq - span), BLOCK)
    kw = k_ref[0, pl.ds(start, span), :].astype(bf16)
    vw = v_ref[0, pl.ds(start, span), :].astype(bf16)
    q = (q_ref[0] * (HEAD_DIM ** -0.5 * LOG2E)).astype(bf16)
    qpos = i * tq + lax.broadcasted_iota(jnp.int32, (tq, span), 0)
    kpos = start + lax.broadcasted_iota(jnp.int32, (tq, span), 1)
    dist = jnp.abs(kpos - qpos)
    valid = dist <= WINDOW
    distf = dist.astype(f32)
    group = WIN_HEADS // WIN_KV_HEADS
    heads = list(range(WIN_HEADS))
    kv_of = lambda x, head: x[:, (head // group) * HEAD_DIM:(head // group + 1) * HEAD_DIM]
    sinks = [sink_ref[head] * LOG2E for head in heads]
    s = [_dot_nt(q[:, head * HEAD_DIM:(head + 1) * HEAD_DIM], kv_of(kw, head)) for head in heads]
    s = [jnp.where(valid, x - (SLOPES_WIN[head] * LOG2E) * distf, -jnp.inf) for head, x in zip(heads, s)]
    m = _each(lambda x, sk: jnp.maximum(jnp.max(x, axis=-1, keepdims=True), sk), s, sinks)
    e = _each(lambda x, mm: jnp.exp2(x - mm), s, m)
    denom = _each(lambda x, mm, sk: jnp.sum(x, axis=-1, keepdims=True) + jnp.exp2(sk - mm), e, m, sinks)
    pv = [jnp.dot(x.astype(bf16), kv_of(vw, head), preferred_element_type=f32) for head, x in zip(heads, e)]
    outs = _each(lambda x, dd: x / dd, pv, denom)
    o_ref[0] = jnp.concatenate(outs, axis=-1).astype(o_ref.dtype)


def window_attention(proj, sink, tq):
    b, t, _ = proj.shape
    kvw = WIN_KV_HEADS * HEAD_DIM
    return pl.pallas_call(
        functools.partial(_win_kernel, seq=t, tq=tq),
        grid=(b, t // tq),
        in_specs=[pl.BlockSpec(memory_space=pltpu.SMEM),
                  pl.BlockSpec((1, tq, WIN_WIDTH), lambda bi, i: (bi, i, OFF_AQ // WIN_WIDTH)),
                  pl.BlockSpec((1, t, kvw), lambda bi, i: (bi, 0, OFF_AK // kvw)),
                  pl.BlockSpec((1, t, kvw), lambda bi, i: (bi, 0, OFF_AV // kvw))],
        out_specs=pl.BlockSpec((1, tq, WIN_WIDTH), lambda bi, i: (bi, i, 0)),
        out_shape=jax.ShapeDtypeStruct((b, t, WIN_WIDTH), bf16),
        compiler_params=_params("parallel", "arbitrary"),
        name="window_attention",
    )(sink, proj, proj, proj)


def _diff_kernel(slope_ref, lq1_ref, lk1_ref, lq2_ref, lk2_ref, g_ref, dist_ref, q_ref, k_ref, v_ref, o_ref,
                 s_ref, e_ref, vt_ref, mx_ref, *, seq, tq, lam_init):
    h = pl.program_id(1)
    i = pl.program_id(2)
    nq = seq // tq
    nkb = seq // BLOCK
    groups = BLOCK // 8
    kt = 2 * BLOCK

    @pl.when(i == 0)
    def _():
        r = lax.broadcasted_iota(jnp.int32, (DIFF_V_DIM, DIFF_V_DIM), 0)
        c = lax.broadcasted_iota(jnp.int32, (DIFF_V_DIM, DIFF_V_DIM), 1)
        eye = jnp.where(r == c, 1.0, 0.0).astype(bf16)
        vt_ref[...] = _dot_nt(eye, v_ref[0]).astype(bf16)

    def step(do_a, do_b, slot_a):
        slot_b = 1 - slot_a
        if do_a:
            q = q_ref[0] * (DIFF_QK_DIM ** -0.5 * LOG2E)
            lane = lax.broadcasted_iota(jnp.int32, q.shape, 1)
            qm = [jnp.where((lane // DIFF_QK_DIM) == c, q, 0.0).astype(bf16) for c in range(2)]
            first = (nq - 1 - i) * (tq // BLOCK)
            slope = slope_ref[h] * LOG2E
            mx = [None, None]
        if do_b:
            m = [jnp.max(mx_ref[slot_b, c], axis=0, keepdims=True) for c in range(2)]
            sm = [None, None]
            acc = [None, None]
        for kb in range(nkb):
            rows = slice(kb * BLOCK, (kb + 1) * BLOCK)
            if do_a:
                kblk = k_ref[0, rows, :]
                bias = slope * dist_ref[first + kb]
                for c in range(2):
                    s = _dot_nt(kblk, qm[c]) - bias
                    s_ref[slot_a, c, rows, :] = s
                    part = jnp.max(s.reshape(groups, 8, tq), axis=0)
                    mx[c] = part if mx[c] is None else jnp.maximum(mx[c], part)
            if do_b:
                for c in range(2):
                    e = jnp.exp2(s_ref[slot_b, c, rows, :] - m[c])
                    e_ref[c, rows, :] = e.astype(bf16)
                    part = jnp.sum(e.reshape(groups, 8, tq), axis=0)
                    sm[c] = part if sm[c] is None else sm[c] + part
                if (kb + 1) % (kt // BLOCK) == 0:
                    keys = slice((kb + 1) * BLOCK - kt, (kb + 1) * BLOCK)
                    for c in range(2):
                        part = jnp.dot(vt_ref[:, keys], e_ref[c, keys, :], preferred_element_type=f32)
                        acc[c] = part if acc[c] is None else acc[c] + part
        if do_a:
            for c in range(2):
                mx_ref[slot_a, c] = mx[c]
        if do_b:
            lam = (jnp.exp(jnp.sum(lq1_ref[...] * lk1_ref[...], axis=-1, keepdims=True))
                   - jnp.exp(jnp.sum(lq2_ref[...] * lk2_ref[...], axis=-1, keepdims=True)) + lam_init)
            outs = [acc[c] / jnp.sum(sm[c], axis=0, keepdims=True) for c in range(2)]
            ot = outs[0] - lam * outs[1]
            ms = jnp.mean(ot * ot, axis=0, keepdims=True)
            ot = ot * lax.rsqrt(ms + RMS_EPS) * g_ref[...] * (1.0 - lam_init)
            o_ref[0] = ot.T.astype(o_ref.dtype)

    middle = (i > 0) & (i < nq)
    pl.when(i == 0)(lambda: step(True, False, 0))
    pl.when(middle & (i % 2 == 0))(lambda: step(True, True, 0))
    pl.when(middle & (i % 2 == 1))(lambda: step(True, True, 1))
    pl.when(i == nq)(lambda: step(False, True, nq % 2))


def _distance_table(seq, tq):
    off = (seq // tq - 1) * (tq // BLOCK)
    nj = seq // BLOCK + off
    j = lax.broadcasted_iota(jnp.int32, (nj, BLOCK, tq), 0)
    kl = lax.broadcasted_iota(jnp.int32, (nj, BLOCK, tq), 1)
    q = lax.broadcasted_iota(jnp.int32, (nj, BLOCK, tq), 2)
    return jnp.abs((j - off) * BLOCK + kl - q).astype(f32)


def diff_attention(qa, kv, lq1, lk1, lq2, lk2, subln, lam_init, tq):
    b, t, _ = qa.shape
    w = DIFF_V_DIM
    vec = pl.BlockSpec((1, DIFF_QK_DIM), lambda bi, h, i: (0, 0))
    slopes = jnp.asarray(SLOPES_DIFF, f32)
    dist = _distance_table(t, tq)
    nq = t // tq
    return pl.pallas_call(
        functools.partial(_diff_kernel, seq=t, tq=tq, lam_init=lam_init),
        grid=(b, DIFF_HEADS, nq + 1),
        in_specs=[pl.BlockSpec(memory_space=pltpu.SMEM), vec, vec, vec, vec,
                  pl.BlockSpec((w, 1), lambda bi, h, i: (0, 0)),
                  pl.BlockSpec(dist.shape, lambda bi, h, i: (0, 0, 0)),
                  pl.BlockSpec((1, tq, w), lambda bi, h, i: (bi, jnp.minimum(i, nq - 1), OFF_BQ // w + h)),
                  pl.BlockSpec((1, t, w), lambda bi, h, i: (bi, 0, (OFF_BK - IN_GROUP) // w + h)),
                  pl.BlockSpec((1, t, w), lambda bi, h, i: (bi, 0, (OFF_BV - IN_GROUP) // w + h))],
        out_specs=pl.BlockSpec((1, tq, w), lambda bi, h, i: (bi, jnp.maximum(i - 1, 0), h)),
        out_shape=jax.ShapeDtypeStruct((b, t, DIFF_WIDTH), bf16),
        scratch_shapes=[pltpu.VMEM((2, 2, t, tq), f32), pltpu.VMEM((2, t, tq), bf16), pltpu.VMEM((w, t), bf16),
                        pltpu.VMEM((2, 2, 8, tq), f32)],
        compiler_params=_params("parallel", "parallel", "arbitrary"),
        name="diff_attention",
    )(slopes, lq1.reshape(1, -1), lk1.reshape(1, -1), lq2.reshape(1, -1), lk2.reshape(1, -1),
      subln.reshape(-1, 1), dist, qa, kv, kv)


def _head_ones():
    r = lax.broadcasted_iota(jnp.int32, (RWKV_WIDTH, RWKV_WIDTH), 0) // RWKV_HEAD
    c = lax.broadcasted_iota(jnp.int32, (RWKV_WIDTH, RWKV_WIDTH), 1) // RWKV_HEAD
    return jnp.where(r == c, 1.0, 0.0).astype(bf16)


def _rwkv_prep_kernel(x_ref, xp_ref, xn_ref, gmix_ref, wc_ref, conv_ref, w0_ref, wup_ref, a0_ref, aup_ref, gup_ref,
                      kk_ref, ka_ref, rk_ref,
                      r_out, k_out, v_out, kkn_out, g_out, bonus_out, lw_out, a_out, *, tt):
    i = pl.program_id(1)
    n = pl.num_programs(1)
    gmix = gmix_ref[...]
    h_prev = jnp.where(i > 0, _rms(xp_ref[0], gmix), 0.0)
    h_next = jnp.where(i < n - 1, _rms(xn_ref[0], gmix), 0.0)
    h = jnp.concatenate([h_prev, _rms(x_ref[0], gmix), h_next], axis=0).astype(bf16)
    proj = jnp.dot(h, wc_ref[...], preferred_element_type=f32)
    x = proj[8:tt + 8]
    first = proj[7:8]
    last = proj[tt + 8:tt + 9]
    c = pltpu.roll(x, 1, 0) * conv_ref[0:1, :] + x * conv_ref[1:2, :] + pltpu.roll(x, tt - 1, 0) * conv_ref[2:3, :]
    row8 = lax.broadcasted_iota(jnp.int32, (8, 1), 0)
    head = c[0:8] + jnp.where(row8 == 0, (first - x[tt - 1:tt]) * conv_ref[0:1, :], 0.0)
    tail = c[tt - 8:tt] + jnp.where(row8 == 7, (last - x[0:1]) * conv_ref[2:3, :], 0.0)
    c = jnp.concatenate([head, c[8:tt - 8], tail], axis=0)
    hw = RWKV_WIDTH
    r = c[:, 0:hw]
    k = c[:, hw:2 * hw]
    v = c[:, 2 * hw:3 * hw]
    o = 3 * hw
    wd = c[:, o:o + W_LORA]
    ad = c[:, o + W_LORA:o + W_LORA + A_LORA]
    gd = c[:, o + W_LORA + A_LORA:o + W_LORA + A_LORA + G_LORA]
    ones = _head_ones()
    kscaled = k * kk_ref[...]
    norm = jnp.sqrt(_group_sum(kscaled * kscaled, ones))
    kkn = kscaled / jnp.maximum(norm, 1e-12)
    tw = jnp.tanh(wd)
    rk_sum = None
    for d in range(2):
        z = w0_ref[d:d + 1, :] + _dot3(tw, wup_ref[d])
        lw_out[d, 0] = -math.exp(-0.5) * _sigmoid(z)
        a = _sigmoid(a0_ref[d:d + 1, :] + _dot(ad, aup_ref[d]))
        a_out[d, 0] = a
        kd = k * (1.0 + (a - 1.0) * ka_ref[...])
        rk = r * kd * rk_ref[...]
        rk_sum = rk if d == 0 else rk_sum + rk
    r_out[0] = r
    k_out[0] = k
    v_out[0] = v
    kkn_out[0] = kkn
    g_out[0] = _dot(_sigmoid(gd), gup_ref[...])
    bonus_out[0] = _group_sum(rk_sum, ones) * v


def rwkv_prep(x, g_mix, w_in, conv, w0, w_up, a0, a_up, g_up, k_k, k_a, r_k, tt):
    b, t, d = x.shape
    hw = RWKV_WIDTH
    nblk8 = t // 8
    full = lambda shape: pl.BlockSpec(shape, lambda bi, i: (0,) * len(shape))
    tok = pl.BlockSpec((1, tt, hw), lambda bi, i: (bi, i, 0))
    tok2 = pl.BlockSpec((2, 1, tt, hw), lambda bi, i: (0, bi, i, 0))
    sds = jax.ShapeDtypeStruct((b, t, hw), f32)
    sds2 = jax.ShapeDtypeStruct((2, b, t, hw), f32)
    return pl.pallas_call(
        functools.partial(_rwkv_prep_kernel, tt=tt),
        grid=(b, t // tt),
        in_specs=[pl.BlockSpec((1, tt, d), lambda bi, i: (bi, i, 0)),
                  pl.BlockSpec((1, 8, d), lambda bi, i: (bi, jnp.maximum(i * (tt // 8) - 1, 0), 0)),
                  pl.BlockSpec((1, 8, d), lambda bi, i: (bi, jnp.minimum((i + 1) * (tt // 8), nblk8 - 1), 0)),
                  full((1, d)),
                  pl.BlockSpec((d, RWKV_IN), lambda bi, i: (0, OFF_C // RWKV_IN)),
                  full((3, RWKV_IN)), full((2, hw)), full((2, W_LORA, hw)), full((2, hw)),
                  full((2, A_LORA, hw)), full((G_LORA, hw)), full((1, hw)), full((1, hw)), full((1, hw))],
        out_specs=[tok, tok, tok, tok, tok, tok, tok2, tok2],
        out_shape=[sds, sds, sds, sds, sds, sds, sds2, sds2],
        compiler_params=_params("parallel", "arbitrary"),
        name="rwkv_prep",
    )(x, x, x, g_mix.reshape(1, d), w_in, conv, w0, w_up, a0, a_up, g_up,
      k_k.reshape(1, hw), k_a.reshape(1, hw), r_k.reshape(1, hw))


def _head_masks():
    lane_head = lax.broadcasted_iota(jnp.int32, (CHUNK, WIDE), 1) // RWKV_HEAD
    return [jnp.where(lane_head == h, 1.0, 0.0).astype(bf16) for h in range(RWKV_HEADS)]


def _stack(x, masks):
    xb = x.astype(bf16)
    return jnp.concatenate([xb * mk for mk in masks], axis=0)


def _mm(a, b, masks):
    return _dot(a, _stack(b, masks))


def _unit_triangular_inverse(ms, eye, masks):
    r = lax.broadcasted_iota(jnp.int32, eye.shape, 0) // INV_BLOCK
    c = (lax.broadcasted_iota(jnp.int32, eye.shape, 1) % CHUNK) // INV_BLOCK
    mm = lambda a, b: _mm(a, b, masks)
    d = [jnp.where(r == c, m, 0.0) for m in ms]
    off = _each(lambda m, dd: m - dd, ms, d)
    x = [-dd for dd in d]
    x2 = _each(lambda xx: mm(xx, xx), x)
    x4 = _each(lambda xx: mm(xx, xx), x2)
    p1 = _each(lambda u, w: mm(eye + u, eye + w), x, x2)
    td = _each(lambda u, w: mm(u, eye + w), p1, x4)
    z = _each(mm, td, off)
    z2 = _each(lambda zz: mm(zz, zz), z)
    z4 = _each(lambda zz: mm(zz, zz), z2)
    p3 = _each(lambda u, w: mm(eye - u, eye + w), z, z2)
    p4 = _each(lambda w, t: mm(eye + w, t), z4, td)
    return _each(mm, p3, p4)


def _rwkv_chunks(revs, lws, a_s, ks, kkns, vs, rs, k_a, h_refs):
    masks = _head_masks()
    ci = lax.broadcasted_iota(jnp.int32, (CHUNK, CHUNK), 0)
    cj = lax.broadcasted_iota(jnp.int32, (CHUNK, CHUNK), 1)
    tri_f = jnp.where(cj <= ci, 1.0, 0.0).astype(bf16)
    tri_r = jnp.where(cj >= ci, 1.0, 0.0).astype(bf16)
    ti = lax.broadcasted_iota(jnp.int32, (CHUNK, WIDE), 0)
    tj = lax.broadcasted_iota(jnp.int32, (CHUNK, WIDE), 1) % CHUNK
    eye = jnp.where(ti == tj, 1.0, 0.0)
    strict = [(tj > ti) if rev else (tj < ti) for rev in revs]
    incl = [(tj >= ti) if rev else (tj <= ti) for rev in revs]

    g_incl = [_dot_exact_rhs(tri_r if rev else tri_f, lw) for rev, lw in zip(revs, lws)]
    g_tot = [g[0:1, :] if rev else g[CHUNK - 1:CHUNK, :] for rev, g in zip(revs, g_incl)]
    kat = _each(lambda kkn, a: kkn * a, kkns, a_s)
    kd = _each(lambda k, a: k * (1.0 + (a - 1.0) * k_a), ks, a_s)
    e_neg = [jnp.exp(-g) for g in g_incl]
    e_rem = _each(lambda gt, g: jnp.exp(gt - g), g_tot, g_incl)
    a_t = _each(lambda kkn, g, lw: kkn * jnp.exp(g - lw), kkns, g_incl, lws)
    r_t = _each(lambda r, g: r * jnp.exp(g), rs, g_incl)
    b_t = _each(lambda x, e: x * e, kat, e_neg)
    k_t = _each(lambda x, e: x * e, kd, e_neg)
    b_hat = _each(lambda x, e: x * e, kat, e_rem)
    k_hat = _each(lambda x, e: x * e, kd, e_rem)

    ar = _each(lambda x, y: jnp.concatenate([x, y], axis=0).astype(bf16), a_t, r_t)
    s_b = _each(lambda x, y: _dot_nt(x, _stack(y, masks)), ar, b_t)
    s_k = _each(lambda x, y: _dot_nt(x, _stack(y, masks)), ar, k_t)
    m_ab = _each(lambda s, x: jnp.where(s, x[:CHUNK], 0.0), strict, s_b)
    a_rb = _each(lambda s, x: jnp.where(s, x[CHUNK:], 0.0), incl, s_b)
    m_ak = _each(lambda s, x: jnp.where(s, x[:CHUNK], 0.0), strict, s_k)
    a_rk = _each(lambda s, x: jnp.where(s, x[CHUNK:], 0.0), incl, s_k)
    t_inv = _unit_triangular_inverse(m_ab, eye, masks)

    wi = lax.broadcasted_iota(jnp.int32, (WIDE, WIDE), 0)
    wj = lax.broadcasted_iota(jnp.int32, (WIDE, WIDE), 1)
    same_head = (wi // RWKV_HEAD) == (wj // RWKV_HEAD)
    diag = wi == wj
    h = [h_ref[...].astype(bf16) for h_ref in h_refs]
    decay = [jnp.where(diag, jnp.exp(gt), 0.0).astype(bf16) for gt in g_tot]
    arh = _each(lambda x, dd, hh: _dot(jnp.concatenate([x, dd], axis=0), hh), ar, decay, h)
    v_s = _each(lambda x: _stack(x, masks), vs)
    mkv = _each(lambda m, n, vv: _dot(jnp.concatenate([m, n], axis=0), vv), m_ak, a_rk, v_s)
    u = _each(lambda t, x, y: -_mm(t, x[:CHUNK] + y[:CHUNK], masks), t_inv, arh, mkv)
    ys = _each(lambda x, m, uu, y: x[CHUNK:2 * CHUNK] + _mm(m, uu, masks) + y[CHUNK:], arh, a_rb, u, mkv)
    bk_t = _each(lambda x, y: jnp.concatenate([x, y], axis=0).T, b_hat, k_hat)
    for h_ref, x, bk, uu, vv in zip(h_refs, arh, bk_t, u, vs):
        upd = _dot(bk, jnp.concatenate([uu, vv], axis=0))
        h_ref[...] = x[2 * CHUNK:] + jnp.where(same_head, upd, 0.0)
    return ys


def _rwkv_scan_kernel(ka_ref,
                      lw0, a0, k0, kk0, v0, r0,
                      lw1, a1, k1, kk1, v1, r1,
                      y0_ref, y1_ref, h_ref, *, rows):
    @pl.when(pl.program_id(1) == 0)
    def _():
        h_ref[...] = jnp.zeros_like(h_ref)

    revs = [False] * rows + [True] * rows
    gather = lambda f, b: [f[0, g] for g in range(rows)] + [b[0, g] for g in range(rows)]
    plain = lambda f, b: [f[g] for g in range(rows)] + [b[g] for g in range(rows)]
    ys = _rwkv_chunks(revs, gather(lw0, lw1), gather(a0, a1), plain(k0, k1), plain(kk0, kk1),
                      plain(v0, v1), plain(r0, r1), ka_ref[...], [h_ref.at[i] for i in range(2 * rows)])
    for g in range(rows):
        y0_ref[g] = ys[g]
        y1_ref[g] = ys[rows + g]


def rwkv_scan(lw, a, k, kkn, v, r, k_a, rows):
    b, t, hw = k.shape
    nc = t // CHUNK
    fwd = pl.BlockSpec((rows, CHUNK, hw), lambda bi, c: (bi, c, 0))
    bwd = pl.BlockSpec((rows, CHUNK, hw), lambda bi, c: (bi, nc - 1 - c, 0))
    fwd2 = pl.BlockSpec((1, rows, CHUNK, hw), lambda bi, c: (0, bi, c, 0))
    bwd2 = pl.BlockSpec((1, rows, CHUNK, hw), lambda bi, c: (1, bi, nc - 1 - c, 0))
    sds = jax.ShapeDtypeStruct((b, t, hw), f32)
    return pl.pallas_call(
        functools.partial(_rwkv_scan_kernel, rows=rows),
        grid=(b // rows, nc),
        in_specs=[pl.BlockSpec((1, hw), lambda bi, c: (0, 0)),
                  fwd2, fwd2, fwd, fwd, fwd, fwd,
                  bwd2, bwd2, bwd, bwd, bwd, bwd],
        out_specs=[fwd, bwd],
        out_shape=[sds, sds],
        scratch_shapes=[pltpu.VMEM((2 * rows, WIDE, WIDE), f32)],
        compiler_params=_params("parallel", "arbitrary"),
        name="rwkv_scan",
    )(k_a.reshape(1, hw), lw, a, k, kkn, v, r, lw, a, k, kkn, v, r)


def _rwkv_finish(y0, y1, g, bonus, ln_g, ln_b):
    ones = _head_ones()
    y = y0 + y1
    mu = _group_sum(y, ones) * (1.0 / RWKV_HEAD)
    yc = y - mu
    var = _group_sum(yc * yc, ones) * (1.0 / RWKV_HEAD)
    yn = yc * lax.rsqrt(var + GN_EPS) * ln_g + ln_b
    return (yn + bonus) * g


def _mix_cross_kernel(x_ref, a_ref, b_ref, y0_ref, y1_ref, g_ref, bonus_ref, lng_ref, lnb_ref,
                      wa_ref, wb_ref, wc_ref, gx_ref, wq_ref, kv_ref, wo_ref, o_ref):
    c = _rwkv_finish(y0_ref[0], y1_ref[0], g_ref[0], bonus_ref[0], lng_ref[...], lnb_ref[...])
    acc = jnp.dot(a_ref[0], wa_ref[...], preferred_element_type=f32)
    acc = acc + jnp.dot(b_ref[0], wb_ref[...], preferred_element_type=f32)
    acc = acc + jnp.dot(c.astype(bf16), wc_ref[...], preferred_element_type=f32)
    x = x_ref[0] + acc
    q = jnp.dot(_rms(x, gx_ref[...]).astype(bf16), wq_ref[...], preferred_element_type=f32)
    q = (q * (X_HEAD_DIM ** -0.5 * LOG2E)).astype(bf16)
    kv = kv_ref[0]
    xw = X_HEADS * X_HEAD_DIM
    heads = range(X_HEADS)
    s = [_dot_nt(q[:, h * X_HEAD_DIM:(h + 1) * X_HEAD_DIM], kv[:, h * X_HEAD_DIM:(h + 1) * X_HEAD_DIM])
         for h in heads]
    e = _each(lambda z: jnp.exp2(z - jnp.max(z, axis=-1, keepdims=True)), s)
    denom = _each(lambda z: jnp.sum(z, axis=-1, keepdims=True), e)
    pv = [jnp.dot(z.astype(bf16), kv[:, xw + h * X_HEAD_DIM:xw + (h + 1) * X_HEAD_DIM],
                  preferred_element_type=f32) for h, z in zip(heads, e)]
    o = jnp.concatenate(_each(lambda z, dd: z / dd, pv, denom), axis=-1)
    o_ref[0] = x + jnp.dot(o.astype(bf16), wo_ref[...], preferred_element_type=f32)


def mix_cross(x, o_win, o_diff, y0, y1, g, bonus, ln_g, ln_b, w_out, kv, g_cross, wq, wo, tm):
    b, t, d = x.shape
    nm = kv.shape[1]
    hw = RWKV_WIDTH
    xw = X_HEADS * X_HEAD_DIM
    wa = w_out[:WIN_WIDTH]
    wb = w_out[WIN_WIDTH:WIN_WIDTH + DIFF_WIDTH]
    wc = w_out[WIN_WIDTH + DIFF_WIDTH:]
    row = lambda w: pl.BlockSpec((1, tm, w), lambda bi, i: (bi, i, 0))
    full = lambda r, c: pl.BlockSpec((r, c), lambda bi, i: (0, 0))
    return pl.pallas_call(
        _mix_cross_kernel,
        grid=(b, t // tm),
        in_specs=[row(d), row(WIN_WIDTH), row(DIFF_WIDTH), row(hw), row(hw), row(hw), row(hw),
                  full(1, hw), full(1, hw), full(WIN_WIDTH, d), full(DIFF_WIDTH, d), full(hw, d),
                  full(1, d), full(d, xw),
                  pl.BlockSpec((1, nm, 2 * xw), lambda bi, i: (bi, 0, 0)),
                  full(xw, d)],
        out_specs=row(d),
        out_shape=jax.ShapeDtypeStruct((b, t, d), f32),
        compiler_params=_params("parallel", "parallel"),
        name="mix_cross",
    )(x, o_win, o_diff, y0, y1, g, bonus, ln_g.reshape(1, hw), ln_b.reshape(1, hw), wa, wb, wc,
      g_cross.reshape(1, d), wq, kv, wo)


def _ffn_up_kernel(x_ref, g_ref, wg_ref, wu_ref, o_ref, h_ref):
    @pl.when(pl.program_id(1) == 0)
    def _():
        h_ref[...] = _rms(x_ref[...], g_ref[...]).astype(bf16)

    h = h_ref[...]
    gate = jnp.dot(h, wg_ref[...], preferred_element_type=f32)
    up = jnp.dot(h, wu_ref[...], preferred_element_type=f32)
    o_ref[...] = (gate * _sigmoid(gate) * up).astype(o_ref.dtype)


def ffn_up(x, g, w_gu, tm, tf):
    m, d = x.shape
    nf = D_FF // tf
    return pl.pallas_call(
        _ffn_up_kernel,
        grid=(m // tm, nf),
        in_specs=[pl.BlockSpec((tm, d), lambda i, j: (i, 0)),
                  pl.BlockSpec((1, d), lambda i, j: (0, 0)),
                  pl.BlockSpec((d, tf), lambda i, j: (0, j)),
                  pl.BlockSpec((d, tf), lambda i, j: (0, j + nf))],
        out_specs=pl.BlockSpec((tm, tf), lambda i, j: (i, j)),
        out_shape=jax.ShapeDtypeStruct((m, D_FF), bf16),
        scratch_shapes=[pltpu.VMEM((tm, d), bf16)],
        compiler_params=_params("parallel", "arbitrary"),
        name="ffn_up",
    )(x, g.reshape(1, d), w_gu, w_gu)


def _ffn_down_kernel(x_ref, a_ref, w_ref, o_ref):
    o_ref[...] = x_ref[...] + jnp.dot(a_ref[...], w_ref[...], preferred_element_type=f32)


def _ffn_down_norm_kernel(x_ref, a_ref, w_ref, g_ref, o_ref):
    y = x_ref[...] + jnp.dot(a_ref[...], w_ref[...], preferred_element_type=f32)
    o_ref[...] = _rms(y, g_ref[...])


def ffn_down(x, act, w_down, tm, final_gain=None):
    m, d = x.shape
    kf = act.shape[1]
    in_specs = [pl.BlockSpec((tm, d), lambda i: (i, 0)),
                pl.BlockSpec((tm, kf), lambda i: (i, 0)),
                pl.BlockSpec((kf, d), lambda i: (0, 0))]
    args = [x, act, w_down]
    if final_gain is not None:
        in_specs.append(pl.BlockSpec((1, d), lambda i: (0, 0)))
        args.append(final_gain.reshape(1, d))
    return pl.pallas_call(
        _ffn_down_kernel if final_gain is None else _ffn_down_norm_kernel,
        grid=(m // tm,),
        in_specs=in_specs,
        out_specs=pl.BlockSpec((tm, d), lambda i: (i, 0)),
        out_shape=jax.ShapeDtypeStruct((m, d), f32),
        compiler_params=_params("parallel"),
        name="ffn_down",
    )(*args)


def _row_tile(m, want):
    t = min(want, m)
    while m % t:
        t //= 2
    return t


def kernel(x, mem, norm_mix, w_in, win_sink, diff_lq1, diff_lk1, diff_lq2, diff_lk2, diff_subln,
           rwkv_conv, rwkv_w0, rwkv_w_up, rwkv_a0, rwkv_a_up, rwkv_g_up, rwkv_k_k, rwkv_k_a, rwkv_r_k,
           rwkv_ln_g, rwkv_ln_b, w_out, norm_cross, norm_mem, x_wq, x_wkv, x_wo, norm_ffn,
           ffn_w_gu, ffn_w_down, norm_final):
    b, t, d = x.shape
    nm = mem.shape[1]
    m = b * t
    tm = _row_tile(m, 1024)
    depth = w_in.shape[0]
    w_in_b, w_out_b, wq_b, wkv_b, wo_b, wgu_b, wdn_b = (
        z.astype(bf16) for z in (w_in, w_out, x_wq, x_wkv, x_wo, ffn_w_gu, ffn_w_down))
    mem2 = mem.reshape(b * nm, d)
    xf = x.reshape(m, d)
    for l in range(depth):
        qa, kvb = (z.reshape(b, t, IN_GROUP) for z in in_proj(xf, norm_mix[l], w_in_b[l], tm))
        o_win = window_attention(qa, win_sink[l], _row_tile(t, 256))
        lam_init = 0.8 - 0.6 * math.exp(-0.3 * l)
        o_diff = diff_attention(qa, kvb, diff_lq1[l], diff_lk1[l], diff_lq2[l], diff_lk2[l], diff_subln[l],
                                lam_init, _row_tile(t, 512))
        r, k, v, kkn, g, bonus, lw, a = rwkv_prep(
            xf.reshape(b, t, d), norm_mix[l], w_in_b[l], rwkv_conv[l], rwkv_w0[l], rwkv_w_up[l], rwkv_a0[l],
            rwkv_a_up[l], rwkv_g_up[l], rwkv_k_k[l], rwkv_k_a[l], rwkv_r_k[l], _row_tile(t, 512))
        y0, y1 = rwkv_scan(lw, a, k, kkn, v, r, rwkv_k_a[l], 4 if b % 4 == 0 else 1)
        kv = norm_matmul(mem2, norm_mem, wkv_b[l], _row_tile(b * nm, 512), 1024, bf16).reshape(b, nm, -1)
        xf = mix_cross(xf.reshape(b, t, d), o_win, o_diff, y0, y1, g, bonus, rwkv_ln_g[l], rwkv_ln_b[l],
                       w_out_b[l], kv, norm_cross[l], wq_b[l], wo_b[l], _row_tile(t, 512)).reshape(m, d)
        act = ffn_up(xf, norm_ffn[l], wgu_b[l], tm, D_FF // 2)
        xf = ffn_down(xf, act, wdn_b[l], tm, norm_final if l == depth - 1 else None)
    return xf.reshape(b, t, d)
```

```python
import functools
import math

import jax
import jax.numpy as jnp
from jax import lax
from jax.experimental import pallas as pl
from jax.experimental.pallas import tpu as pltpu

f32 = jnp.float32
bf16 = jnp.bfloat16

D_MODEL = 1024
DEPTH = 4
HEAD_DIM = 64
BLOCK = 128
WINDOW = 128
WIN_HEADS = 4
WIN_KV_HEADS = 2
DIFF_HEADS = 4
DIFF_QK_DIM = 64
DIFF_V_DIM = 2 * DIFF_QK_DIM
RWKV_HEADS = 4
RWKV_HEAD = 64
W_LORA = 64
A_LORA = 64
G_LORA = 128
X_HEADS = 4
X_HEAD_DIM = 128
D_FF = -(-8 * D_MODEL // (3 * 256)) * 256
RMS_EPS = 1e-6
GN_EPS = 64e-5

WIN_WIDTH = WIN_HEADS * HEAD_DIM
DIFF_WIDTH = DIFF_HEADS * DIFF_V_DIM
RWKV_WIDTH = RWKV_HEADS * RWKV_HEAD
RWKV_IN = 3 * RWKV_WIDTH + W_LORA + A_LORA + G_LORA
IN_WIDTH = 2 * WIN_WIDTH + 3 * DIFF_WIDTH + RWKV_IN

OFF_AQ = 0
OFF_AK = WIN_WIDTH
OFF_AV = OFF_AK + WIN_KV_HEADS * HEAD_DIM
OFF_BQ = OFF_AV + WIN_KV_HEADS * HEAD_DIM
OFF_BK = OFF_BQ + DIFF_WIDTH
OFF_BV = OFF_BK + DIFF_WIDTH
OFF_C = OFF_BV + DIFF_WIDTH
IN_GROUP = 1024
assert OFF_BK == IN_GROUP and OFF_C == 2 * IN_GROUP and IN_WIDTH == 3 * IN_GROUP and RWKV_IN == IN_GROUP

CHUNK = 64
INV_BLOCK = 8
assert INV_BLOCK == 8 and CHUNK == 8 * INV_BLOCK
WIDE = RWKV_HEADS * CHUNK
V7X_VMEM_LIMIT = 56 * 1024 * 1024
LOG2E = 1.4426950408889634
SOFTMAX_ROWS = 16

_SLOPES = [2.0 ** (-8.0 * i / (WIN_HEADS + DIFF_HEADS)) for i in range(1, WIN_HEADS + DIFF_HEADS + 1)]
SLOPES_WIN = _SLOPES[0::2]
SLOPES_DIFF = _SLOPES[1::2]


def _params(*sem):
    return pltpu.CompilerParams(dimension_semantics=sem, vmem_limit_bytes=V7X_VMEM_LIMIT)


def _dot(a, b):
    return jnp.dot(a.astype(bf16), b.astype(bf16), preferred_element_type=f32)


def _dot_nt(a, b):
    return lax.dot_general(a.astype(bf16), b.astype(bf16), (((1,), (1,)), ((), ())),
                           preferred_element_type=f32)


def _each(fn, *lists):
    return [fn(*args) for args in zip(*lists)]


def _split2(a):
    hi = a.astype(bf16)
    lo = (a - hi.astype(f32)).astype(bf16)
    return hi, lo


def _split3(a):
    hi = a.astype(bf16)
    r1 = a - hi.astype(f32)
    mid = r1.astype(bf16)
    lo = (r1 - mid.astype(f32)).astype(bf16)
    return hi, mid, lo


def _dot3(a, b, nt=False):
    d = _dot_nt if nt else _dot
    ah, al = _split2(a)
    bh, bl = _split2(b)
    return d(ah, bh) + (d(ah, bl) + d(al, bh))


def _dot_exact_rhs(a_bf16, b):
    b1, b2, b3 = _split3(b)
    return (jnp.dot(a_bf16, b1, preferred_element_type=f32)
            + (jnp.dot(a_bf16, b2, preferred_element_type=f32)
               + jnp.dot(a_bf16, b3, preferred_element_type=f32)))


def _group_sum(a, ones_bf16):
    a1, a2 = _split2(a)
    return (jnp.dot(a1, ones_bf16, preferred_element_type=f32)
            + jnp.dot(a2, ones_bf16, preferred_element_type=f32))


def _rms(x, g):
    ms = jnp.mean(x * x, axis=-1, keepdims=True)
    return (x * lax.rsqrt(ms + RMS_EPS)) * g


def _sigmoid(z):
    return 1.0 / (1.0 + jnp.exp(-z))


def _norm_matmul_kernel(x_ref, g_ref, w_ref, o_ref, h_ref):
    @pl.when(pl.program_id(1) == 0)
    def _():
        h_ref[...] = _rms(x_ref[...], g_ref[...]).astype(bf16)

    o_ref[...] = jnp.dot(h_ref[...], w_ref[...], preferred_element_type=f32).astype(o_ref.dtype)


def norm_matmul(x, g, w, tm, tn, out_dtype=f32):
    m, k = x.shape
    n = w.shape[1]
    return pl.pallas_call(
        _norm_matmul_kernel,
        grid=(m // tm, n // tn),
        in_specs=[pl.BlockSpec((tm, k), lambda i, j: (i, 0)),
                  pl.BlockSpec((1, k), lambda i, j: (0, 0)),
                  pl.BlockSpec((k, tn), lambda i, j: (0, j))],
        out_specs=pl.BlockSpec((tm, tn), lambda i, j: (i, j)),
        out_shape=jax.ShapeDtypeStruct((m, n), out_dtype),
        scratch_shapes=[pltpu.VMEM((tm, k), bf16)],
        compiler_params=_params("parallel", "arbitrary"),
        name="norm_matmul",
    )(x, g.reshape(1, k), w)


def _in_proj_kernel(x_ref, g_ref, w_ref, oa_ref, ob_ref, h_ref):
    j = pl.program_id(1)

    @pl.when(j == 0)
    def _():
        h_ref[...] = _rms(x_ref[...], g_ref[...]).astype(bf16)

    for idx, ref in enumerate((oa_ref, ob_ref)):
        @pl.when(j == idx)
        def _(ref=ref):
            ref[...] = jnp.dot(h_ref[...], w_ref[...], preferred_element_type=f32).astype(ref.dtype)


def in_proj(x, g, w, layer, tm):
    m, k = x.shape
    assert w.shape[2] == 3 * IN_GROUP
    out = pl.BlockSpec((tm, IN_GROUP), lambda i, j: (i, 0))
    return pl.pallas_call(
        _in_proj_kernel,
        grid=(m // tm, 2),
        in_specs=[pl.BlockSpec((tm, k), lambda i, j: (i, 0)),
                  pl.BlockSpec((1, k), lambda i, j: (0, 0)),
                  pl.BlockSpec((None, k, IN_GROUP), lambda i, j: (layer, 0, j))],
        out_specs=[out, out],
        out_shape=[jax.ShapeDtypeStruct((m, IN_GROUP), f32), jax.ShapeDtypeStruct((m, IN_GROUP), bf16)],
        scratch_shapes=[pltpu.VMEM((tm, k), bf16)],
        compiler_params=_params("parallel", "arbitrary"),
        name="in_proj",
    )(x, g.reshape(1, k), w)


def _win_kernel(sink_ref, q_ref, k_ref, v_ref, o_ref, *, seq, tq):
    i = pl.program_id(1)
    span = tq + 2 * BLOCK
    start = pl.multiple_of(jnp.clip(i * tq - BLOCK, 0, seq - span), BLOCK)
    kw = k_ref[0, pl.ds(start, span), :].astype(bf16)
    vw = v_ref[0, pl.ds(start, span), :].astype(bf16)
    q = (q_ref[0] * (HEAD_DIM ** -0.5 * LOG2E)).astype(bf16)
    qpos = i * tq + lax.broadcasted_iota(jnp.int32, (tq, span), 0)
    kpos = start + lax.broadcasted_iota(jnp.int32, (tq, span), 1)
    dist = jnp.abs(kpos - qpos)
    valid = dist <= WINDOW
    distf = dist.astype(f32)
    group = WIN_HEADS // WIN_KV_HEADS
    heads = list(range(WIN_HEADS))
    kv_of = lambda x, head: x[:, (head // group) * HEAD_DIM:(head // group + 1) * HEAD_DIM]
    sinks = [sink_ref[head] * LOG2E for head in heads]
    s = [_dot_nt(q[:, head * HEAD_DIM:(head + 1) * HEAD_DIM], kv_of(kw, head)) for head in heads]
    s = [jnp.where(valid, x - (SLOPES_WIN[head] * LOG2E) * distf, -jnp.inf) for head, x in zip(heads, s)]
    m = _each(lambda x, sk: jnp.maximum(jnp.max(x, axis=-1, keepdims=True), sk), s, sinks)
    e = _each(lambda x, mm: jnp.exp2(x - mm), s, m)
    denom = _each(lambda x, mm, sk: jnp.sum(x, axis=-1, keepdims=True) + jnp.exp2(sk - mm), e, m, sinks)
    pv = [jnp.dot(x.astype(bf16), kv_of(vw, head), preferred_element_type=f32) for head, x in zip(heads, e)]
    outs = _each(lambda x, dd: x / dd, pv, denom)
    o_ref[0] = jnp.concatenate(outs, axis=-1).astype(o_ref.dtype)


def window_attention(proj, sink, tq):
    b, t, _ = proj.shape
    kvw = WIN_KV_HEADS * HEAD_DIM
    return pl.pallas_call(
        functools.partial(_win_kernel, seq=t, tq=tq),
        grid=(b, t // tq),
        in_specs=[pl.BlockSpec(memory_space=pltpu.SMEM),
                  pl.BlockSpec((1, tq, WIN_WIDTH), lambda bi, i: (bi, i, OFF_AQ // WIN_WIDTH)),
                  pl.BlockSpec((1, t, kvw), lambda bi, i: (bi, 0, OFF_AK // kvw)),
                  pl.BlockSpec((1, t, kvw), lambda bi, i: (bi, 0, OFF_AV // kvw))],
        out_specs=pl.BlockSpec((1, tq, WIN_WIDTH), lambda bi, i: (bi, i, 0)),
        out_shape=jax.ShapeDtypeStruct((b, t, WIN_WIDTH), bf16),
        compiler_params=_params("parallel", "arbitrary"),
        name="window_attention",
    )(sink, proj, proj, proj)


def _diff_kernel(slope_ref, lq1_ref, lk1_ref, lq2_ref, lk2_ref, g_ref, dist_ref, q_ref, k_ref, v_ref, o_ref,
                 s_ref, e_ref, vt_ref, mx_ref, *, seq, tq, lam_init):
    h = pl.program_id(1)
    i = pl.program_id(2)
    nq = seq // tq
    nkb = seq // BLOCK
    groups = BLOCK // 8
    kt = 2 * BLOCK

    @pl.when(i == 0)
    def _():
        r = lax.broadcasted_iota(jnp.int32, (DIFF_V_DIM, DIFF_V_DIM), 0)
        c = lax.broadcasted_iota(jnp.int32, (DIFF_V_DIM, DIFF_V_DIM), 1)
        eye = jnp.where(r == c, 1.0, 0.0).astype(bf16)
        vt_ref[...] = _dot_nt(eye, v_ref[0]).astype(bf16)

    def step(do_a, do_b, slot_a):
        slot_b = 1 - slot_a
        if do_a:
            q = q_ref[0] * (DIFF_QK_DIM ** -0.5 * LOG2E)
            lane = lax.broadcasted_iota(jnp.int32, q.shape, 1)
            qm = [jnp.where((lane // DIFF_QK_DIM) == c, q, 0.0).astype(bf16) for c in range(2)]
            first = (nq - 1 - i) * (tq // BLOCK)
            slope = slope_ref[h] * LOG2E
            mx = [None, None]
        if do_b:
            m = [jnp.max(mx_ref[slot_b, c], axis=0, keepdims=True) for c in range(2)]
            sm = [None, None]
            acc = [None, None]
        for kb in range(nkb):
            rows = slice(kb * BLOCK, (kb + 1) * BLOCK)
            if do_a:
                kblk = k_ref[0, rows, :]
                bias = slope * dist_ref[first + kb]
                for c in range(2):
                    s = _dot_nt(kblk, qm[c]) - bias
                    s_ref[slot_a, c, rows, :] = s
                    part = jnp.max(s.reshape(groups, 8, tq), axis=0)
                    mx[c] = part if mx[c] is None else jnp.maximum(mx[c], part)
            if do_b:
                for c in range(2):
                    e = jnp.exp2(s_ref[slot_b, c, rows, :] - m[c])
                    e_ref[c, rows, :] = e.astype(bf16)
                    part = jnp.sum(e.reshape(groups, 8, tq), axis=0)
                    sm[c] = part if sm[c] is None else sm[c] + part
                if (kb + 1) % (kt // BLOCK) == 0:
                    keys = slice((kb + 1) * BLOCK - kt, (kb + 1) * BLOCK)
                    for c in range(2):
                        part = jnp.dot(vt_ref[:, keys], e_ref[c, keys, :], preferred_element_type=f32)
                        acc[c] = part if acc[c] is None else acc[c] + part
        if do_a:
            for c in range(2):
                mx_ref[slot_a, c] = mx[c]
        if do_b:
            lam = (jnp.exp(jnp.sum(lq1_ref[...] * lk1_ref[...], axis=-1, keepdims=True))
                   - jnp.exp(jnp.sum(lq2_ref[...] * lk2_ref[...], axis=-1, keepdims=True)) + lam_init)
            outs = [acc[c] / jnp.sum(sm[c], axis=0, keepdims=True) for c in range(2)]
            ot = outs[0] - lam * outs[1]
            ms = jnp.mean(ot * ot, axis=0, keepdims=True)
            ot = ot * lax.rsqrt(ms + RMS_EPS) * g_ref[...] * (1.0 - lam_init)
            o_ref[0] = ot.T.astype(o_ref.dtype)

    middle = (i > 0) & (i < nq)
    pl.when(i == 0)(lambda: step(True, False, 0))
    pl.when(middle & (i % 2 == 0))(lambda: step(True, True, 0))
    pl.when(middle & (i % 2 == 1))(lambda: step(True, True, 1))
    pl.when(i == nq)(lambda: step(False, True, nq % 2))


def _distance_table(seq, tq):
    off = (seq // tq - 1) * (tq // BLOCK)
    nj = seq // BLOCK + off
    j = lax.broadcasted_iota(jnp.int32, (nj, BLOCK, tq), 0)
    kl = lax.broadcasted_iota(jnp.int32, (nj, BLOCK, tq), 1)
    q = lax.broadcasted_iota(jnp.int32, (nj, BLOCK, tq), 2)
    return jnp.abs((j - off) * BLOCK + kl - q).astype(f32)


def diff_attention(qa, kv, lq1, lk1, lq2, lk2, subln, lam_init, tq):
    b, t, _ = qa.shape
    w = DIFF_V_DIM
    vec = pl.BlockSpec((1, DIFF_QK_DIM), lambda bi, h, i: (0, 0))
    slopes = jnp.asarray(SLOPES_DIFF, f32)
    dist = _distance_table(t, tq)
    nq = t // tq
    return pl.pallas_call(
        functools.partial(_diff_kernel, seq=t, tq=tq, lam_init=lam_init),
        grid=(b, DIFF_HEADS, nq + 1),
        in_specs=[pl.BlockSpec(memory_space=pltpu.SMEM), vec, vec, vec, vec,
                  pl.BlockSpec((w, 1), lambda bi, h, i: (0, 0)),
                  pl.BlockSpec(dist.shape, lambda bi, h, i: (0, 0, 0)),
                  pl.BlockSpec((1, tq, w), lambda bi, h, i: (bi, jnp.minimum(i, nq - 1), OFF_BQ // w + h)),
                  pl.BlockSpec((1, t, w), lambda bi, h, i: (bi, 0, (OFF_BK - IN_GROUP) // w + h)),
                  pl.BlockSpec((1, t, w), lambda bi, h, i: (bi, 0, (OFF_BV - IN_GROUP) // w + h))],
        out_specs=pl.BlockSpec((1, tq, w), lambda bi, h, i: (bi, jnp.maximum(i - 1, 0), h)),
        out_shape=jax.ShapeDtypeStruct((b, t, DIFF_WIDTH), bf16),
        scratch_shapes=[pltpu.VMEM((2, 2, t, tq), f32), pltpu.VMEM((2, t, tq), bf16), pltpu.VMEM((w, t), bf16),
                        pltpu.VMEM((2, 2, 8, tq), f32)],
        compiler_params=_params("parallel", "parallel", "arbitrary"),
        name="diff_attention",
    )(slopes, lq1.reshape(1, -1), lk1.reshape(1, -1), lq2.reshape(1, -1), lk2.reshape(1, -1),
      subln.reshape(-1, 1), dist, qa, kv, kv)


def _head_ones():
    r = lax.broadcasted_iota(jnp.int32, (RWKV_WIDTH, RWKV_WIDTH), 0) // RWKV_HEAD
    c = lax.broadcasted_iota(jnp.int32, (RWKV_WIDTH, RWKV_WIDTH), 1) // RWKV_HEAD
    return jnp.where(r == c, 1.0, 0.0).astype(bf16)


def _rwkv_prep_kernel(x_ref, xp_ref, xn_ref, gmix_ref, wc_ref, conv_ref, w0_ref, wup_ref, a0_ref, aup_ref, gup_ref,
                      kk_ref, ka_ref, rk_ref,
                      r_out, k_out, v_out, kkn_out, g_out, bonus_out, lw_out, a_out, *, tt):
    i = pl.program_id(1)
    n = pl.num_programs(1)
    gmix = gmix_ref[...]
    h_prev = jnp.where(i > 0, _rms(xp_ref[0], gmix), 0.0)
    h_next = jnp.where(i < n - 1, _rms(xn_ref[0], gmix), 0.0)
    h = jnp.concatenate([h_prev, _rms(x_ref[0], gmix), h_next], axis=0).astype(bf16)
    proj = jnp.dot(h, wc_ref[...], preferred_element_type=f32)
    x = proj[8:tt + 8]
    first = proj[7:8]
    last = proj[tt + 8:tt + 9]
    c = pltpu.roll(x, 1, 0) * conv_ref[0:1, :] + x * conv_ref[1:2, :] + pltpu.roll(x, tt - 1, 0) * conv_ref[2:3, :]
    row8 = lax.broadcasted_iota(jnp.int32, (8, 1), 0)
    head = c[0:8] + jnp.where(row8 == 0, (first - x[tt - 1:tt]) * conv_ref[0:1, :], 0.0)
    tail = c[tt - 8:tt] + jnp.where(row8 == 7, (last - x[0:1]) * conv_ref[2:3, :], 0.0)
    c = jnp.concatenate([head, c[8:tt - 8], tail], axis=0)
    hw = RWKV_WIDTH
    r = c[:, 0:hw]
    k = c[:, hw:2 * hw]
    v = c[:, 2 * hw:3 * hw]
    o = 3 * hw
    wd = c[:, o:o + W_LORA]
    ad = c[:, o + W_LORA:o + W_LORA + A_LORA]
    gd = c[:, o + W_LORA + A_LORA:o + W_LORA + A_LORA + G_LORA]
    ones = _head_ones()
    kscaled = k * kk_ref[...]
    norm = jnp.sqrt(_group_sum(kscaled * kscaled, ones))
    kkn = kscaled / jnp.maximum(norm, 1e-12)
    tw = jnp.tanh(wd)
    rk_sum = None
    for d in range(2):
        z = w0_ref[d:d + 1, :] + _dot3(tw, wup_ref[d])
        lw_out[d, 0] = -math.exp(-0.5) * _sigmoid(z)
        a = _sigmoid(a0_ref[d:d + 1, :] + _dot(ad, aup_ref[d]))
        a_out[d, 0] = a
        kd = k * (1.0 + (a - 1.0) * ka_ref[...])
        rk = r * kd * rk_ref[...]
        rk_sum = rk if d == 0 else rk_sum + rk
    r_out[0] = r
    k_out[0] = k
    v_out[0] = v
    kkn_out[0] = kkn
    g_out[0] = _dot(_sigmoid(gd), gup_ref[...])
    bonus_out[0] = _group_sum(rk_sum, ones) * v


def rwkv_prep(x, g_mix, w_in, layer, conv, w0, w_up, a0, a_up, g_up, k_k, k_a, r_k, tt):
    b, t, d = x.shape
    hw = RWKV_WIDTH
    nblk8 = t // 8
    full = lambda shape: pl.BlockSpec(shape, lambda bi, i: (0,) * len(shape))
    tok = pl.BlockSpec((1, tt, hw), lambda bi, i: (bi, i, 0))
    tok2 = pl.BlockSpec((2, 1, tt, hw), lambda bi, i: (0, bi, i, 0))
    sds = jax.ShapeDtypeStruct((b, t, hw), f32)
    sds2 = jax.ShapeDtypeStruct((2, b, t, hw), f32)
    return pl.pallas_call(
        functools.partial(_rwkv_prep_kernel, tt=tt),
        grid=(b, t // tt),
        in_specs=[pl.BlockSpec((1, tt, d), lambda bi, i: (bi, i, 0)),
                  pl.BlockSpec((1, 8, d), lambda bi, i: (bi, jnp.maximum(i * (tt // 8) - 1, 0), 0)),
                  pl.BlockSpec((1, 8, d), lambda bi, i: (bi, jnp.minimum((i + 1) * (tt // 8), nblk8 - 1), 0)),
                  full((1, d)),
                  pl.BlockSpec((None, d, RWKV_IN), lambda bi, i: (layer, 0, OFF_C // RWKV_IN)),
                  full((3, RWKV_IN)), full((2, hw)), full((2, W_LORA, hw)), full((2, hw)),
                  full((2, A_LORA, hw)), full((G_LORA, hw)), full((1, hw)), full((1, hw)), full((1, hw))],
        out_specs=[tok, tok, tok, tok, tok, tok, tok2, tok2],
        out_shape=[sds, sds, sds, sds, sds, sds, sds2, sds2],
        compiler_params=_params("parallel", "arbitrary"),
        name="rwkv_prep",
    )(x, x, x, g_mix.reshape(1, d), w_in, conv, w0, w_up, a0, a_up, g_up,
      k_k.reshape(1, hw), k_a.reshape(1, hw), r_k.reshape(1, hw))


def _head_masks():
    lane_head = lax.broadcasted_iota(jnp.int32, (CHUNK, WIDE), 1) // RWKV_HEAD
    return [jnp.where(lane_head == h, 1.0, 0.0).astype(bf16) for h in range(RWKV_HEADS)]


def _stack(x, masks):
    xb = x.astype(bf16)
    return jnp.concatenate([xb * mk for mk in masks], axis=0)


def _mm(a, b, masks):
    return _dot(a, _stack(b, masks))


def _unit_triangular_inverse(ms, eye, masks):
    r = lax.broadcasted_iota(jnp.int32, eye.shape, 0) // INV_BLOCK
    c = (lax.broadcasted_iota(jnp.int32, eye.shape, 1) % CHUNK) // INV_BLOCK
    mm = lambda a, b: _mm(a, b, masks)
    d = [jnp.where(r == c, m, 0.0) for m in ms]
    off = _each(lambda m, dd: m - dd, ms, d)
    x = [-dd for dd in d]
    x2 = _each(lambda xx: mm(xx, xx), x)
    x4 = _each(lambda xx: mm(xx, xx), x2)
    p1 = _each(lambda u, w: mm(eye + u, eye + w), x, x2)
    td = _each(lambda u, w: mm(u, eye + w), p1, x4)
    z = _each(mm, td, off)
    z2 = _each(lambda zz: mm(zz, zz), z)
    z4 = _each(lambda zz: mm(zz, zz), z2)
    p3 = _each(lambda u, w: mm(eye - u, eye + w), z, z2)
    p4 = _each(lambda w, t: mm(eye + w, t), z4, td)
    return _each(mm, p3, p4)


def _rwkv_chunks(revs, lws, a_s, ks, kkns, vs, rs, k_a, h_refs):
    masks = _head_masks()
    ci = lax.broadcasted_iota(jnp.int32, (CHUNK, CHUNK), 0)
    cj = lax.broadcasted_iota(jnp.int32, (CHUNK, CHUNK), 1)
    tri_f = jnp.where(cj <= ci, 1.0, 0.0).astype(bf16)
    tri_r = jnp.where(cj >= ci, 1.0, 0.0).astype(bf16)
    ti = lax.broadcasted_iota(jnp.int32, (CHUNK, WIDE), 0)
    tj = lax.broadcasted_iota(jnp.int32, (CHUNK, WIDE), 1) % CHUNK
    eye = jnp.where(ti == tj, 1.0, 0.0)
    strict = [(tj > ti) if rev else (tj < ti) for rev in revs]
    incl = [(tj >= ti) if rev else (tj <= ti) for rev in revs]

    g_incl = [_dot_exact_rhs(tri_r if rev else tri_f, lw) for rev, lw in zip(revs, lws)]
    g_tot = [g[0:1, :] if rev else g[CHUNK - 1:CHUNK, :] for rev, g in zip(revs, g_incl)]
    kat = _each(lambda kkn, a: kkn * a, kkns, a_s)
    kd = _each(lambda k, a: k * (1.0 + (a - 1.0) * k_a), ks, a_s)
    e_neg = [jnp.exp(-g) for g in g_incl]
    e_rem = _each(lambda gt, g: jnp.exp(gt - g), g_tot, g_incl)
    a_t = _each(lambda kkn, g, lw: kkn * jnp.exp(g - lw), kkns, g_incl, lws)
    r_t = _each(lambda r, g: r * jnp.exp(g), rs, g_incl)
    b_t = _each(lambda x, e: x * e, kat, e_neg)
    k_t = _each(lambda x, e: x * e, kd, e_neg)
    b_hat = _each(lambda x, e: x * e, kat, e_rem)
    k_hat = _each(lambda x, e: x * e, kd, e_rem)

    ar = _each(lambda x, y: jnp.concatenate([x, y], axis=0).astype(bf16), a_t, r_t)
    s_b = _each(lambda x, y: _dot_nt(x, _stack(y, masks)), ar, b_t)
    s_k = _each(lambda x, y: _dot_nt(x, _stack(y, masks)), ar, k_t)
    m_ab = _each(lambda s, x: jnp.where(s, x[:CHUNK], 0.0), strict, s_b)
    a_rb = _each(lambda s, x: jnp.where(s, x[CHUNK:], 0.0), incl, s_b)
    m_ak = _each(lambda s, x: jnp.where(s, x[:CHUNK], 0.0), strict, s_k)
    a_rk = _each(lambda s, x: jnp.where(s, x[CHUNK:], 0.0), incl, s_k)
    t_inv = _unit_triangular_inverse(m_ab, eye, masks)

    wi = lax.broadcasted_iota(jnp.int32, (WIDE, WIDE), 0)
    wj = lax.broadcasted_iota(jnp.int32, (WIDE, WIDE), 1)
    same_head = (wi // RWKV_HEAD) == (wj // RWKV_HEAD)
    diag = wi == wj
    h = [h_ref[...].astype(bf16) for h_ref in h_refs]
    decay = [jnp.where(diag, jnp.exp(gt), 0.0).astype(bf16) for gt in g_tot]
    arh = _each(lambda x, dd, hh: _dot(jnp.concatenate([x, dd], axis=0), hh), ar, decay, h)
    v_s = _each(lambda x: _stack(x, masks), vs)
    mkv = _each(lambda m, n, vv: _dot(jnp.concatenate([m, n], axis=0), vv), m_ak, a_rk, v_s)
    u = _each(lambda t, x, y: -_mm(t, x[:CHUNK] + y[:CHUNK], masks), t_inv, arh, mkv)
    ys = _each(lambda x, m, uu, y: x[CHUNK:2 * CHUNK] + _mm(m, uu, masks) + y[CHUNK:], arh, a_rb, u, mkv)
    bk_t = _each(lambda x, y: jnp.concatenate([x, y], axis=0).T, b_hat, k_hat)
    for h_ref, x, bk, uu, vv in zip(h_refs, arh, bk_t, u, vs):
        upd = _dot(bk, jnp.concatenate([uu, vv], axis=0))
        h_ref[...] = x[2 * CHUNK:] + jnp.where(same_head, upd, 0.0)
    return ys


def _rwkv_scan_kernel(ka_ref,
                      lw0, a0, k0, kk0, v0, r0,
                      lw1, a1, k1, kk1, v1, r1,
                      y0_ref, y1_ref, h_ref, *, rows):
    @pl.when(pl.program_id(1) == 0)
    def _():
        h_ref[...] = jnp.zeros_like(h_ref)

    revs = [False] * rows + [True] * rows
    gather = lambda f, b: [f[0, g] for g in range(rows)] + [b[0, g] for g in range(rows)]
    plain = lambda f, b: [f[g] for g in range(rows)] + [b[g] for g in range(rows)]
    ys = _rwkv_chunks(revs, gather(lw0, lw1), gather(a0, a1), plain(k0, k1), plain(kk0, kk1),
                      plain(v0, v1), plain(r0, r1), ka_ref[...], [h_ref.at[i] for i in range(2 * rows)])
    for g in range(rows):
        y0_ref[g] = ys[g]
        y1_ref[g] = ys[rows + g]


def rwkv_scan(lw, a, k, kkn, v, r, k_a, rows):
    b, t, hw = k.shape
    nc = t // CHUNK
    fwd = pl.BlockSpec((rows, CHUNK, hw), lambda bi, c: (bi, c, 0))
    bwd = pl.BlockSpec((rows, CHUNK, hw), lambda bi, c: (bi, nc - 1 - c, 0))
    fwd2 = pl.BlockSpec((1, rows, CHUNK, hw), lambda bi, c: (0, bi, c, 0))
    bwd2 = pl.BlockSpec((1, rows, CHUNK, hw), lambda bi, c: (1, bi, nc - 1 - c, 0))
    sds = jax.ShapeDtypeStruct((b, t, hw), f32)
    return pl.pallas_call(
        functools.partial(_rwkv_scan_kernel, rows=rows),
        grid=(b // rows, nc),
        in_specs=[pl.BlockSpec((1, hw), lambda bi, c: (0, 0)),
                  fwd2, fwd2, fwd, fwd, fwd, fwd,
                  bwd2, bwd2, bwd, bwd, bwd, bwd],
        out_specs=[fwd, bwd],
        out_shape=[sds, sds],
        scratch_shapes=[pltpu.VMEM((2 * rows, WIDE, WIDE), f32)],
        compiler_params=_params("parallel", "arbitrary"),
        name="rwkv_scan",
    )(k_a.reshape(1, hw), lw, a, k, kkn, v, r, lw, a, k, kkn, v, r)


def _rwkv_finish(y0, y1, g, bonus, ln_g, ln_b):
    ones = _head_ones()
    y = y0 + y1
    mu = _group_sum(y, ones) * (1.0 / RWKV_HEAD)
    yc = y - mu
    var = _group_sum(yc * yc, ones) * (1.0 / RWKV_HEAD)
    yn = yc * lax.rsqrt(var + GN_EPS) * ln_g + ln_b
    return (yn + bonus) * g


def _mix_cross_kernel(x_ref, a_ref, b_ref, y0_ref, y1_ref, g_ref, bonus_ref, lng_ref, lnb_ref,
                      wa_ref, wb_ref, wc_ref, gx_ref, wq_ref, kv_ref, wo_ref, o_ref):
    c = _rwkv_finish(y0_ref[0], y1_ref[0], g_ref[0], bonus_ref[0], lng_ref[...], lnb_ref[...])
    acc = jnp.dot(a_ref[0], wa_ref[...], preferred_element_type=f32)
    acc = acc + jnp.dot(b_ref[0], wb_ref[...], preferred_element_type=f32)
    acc = acc + jnp.dot(c.astype(bf16), wc_ref[...], preferred_element_type=f32)
    x = x_ref[0] + acc
    q = jnp.dot(_rms(x, gx_ref[...]).astype(bf16), wq_ref[...], preferred_element_type=f32)
    q = (q * (X_HEAD_DIM ** -0.5 * LOG2E)).astype(bf16)
    kv = kv_ref[0]
    xw = X_HEADS * X_HEAD_DIM
    heads = range(X_HEADS)
    s = [_dot_nt(q[:, h * X_HEAD_DIM:(h + 1) * X_HEAD_DIM], kv[:, h * X_HEAD_DIM:(h + 1) * X_HEAD_DIM])
         for h in heads]
    e = _each(lambda z: jnp.exp2(z - jnp.max(z, axis=-1, keepdims=True)), s)
    denom = _each(lambda z: jnp.sum(z, axis=-1, keepdims=True), e)
    pv = [jnp.dot(z.astype(bf16), kv[:, xw + h * X_HEAD_DIM:xw + (h + 1) * X_HEAD_DIM],
                  preferred_element_type=f32) for h, z in zip(heads, e)]
    o = jnp.concatenate(_each(lambda z, dd: z / dd, pv, denom), axis=-1)
    o_ref[0] = x + jnp.dot(o.astype(bf16), wo_ref[...], preferred_element_type=f32)


def mix_cross(x, o_win, o_diff, y0, y1, g, bonus, ln_g, ln_b, w_out, kv, g_cross, wq, wo, tm):
    b, t, d = x.shape
    nm = kv.shape[1]
    hw = RWKV_WIDTH
    xw = X_HEADS * X_HEAD_DIM
    wa = w_out[:WIN_WIDTH]
    wb = w_out[WIN_WIDTH:WIN_WIDTH + DIFF_WIDTH]
    wc = w_out[WIN_WIDTH + DIFF_WIDTH:]
    row = lambda w: pl.BlockSpec((1, tm, w), lambda bi, i: (bi, i, 0))
    full = lambda r, c: pl.BlockSpec((r, c), lambda bi, i: (0, 0))
    return pl.pallas_call(
        _mix_cross_kernel,
        grid=(b, t // tm),
        in_specs=[row(d), row(WIN_WIDTH), row(DIFF_WIDTH), row(hw), row(hw), row(hw), row(hw),
                  full(1, hw), full(1, hw), full(WIN_WIDTH, d), full(DIFF_WIDTH, d), full(hw, d),
                  full(1, d), full(d, xw),
                  pl.BlockSpec((1, nm, 2 * xw), lambda bi, i: (bi, 0, 0)),
                  full(xw, d)],
        out_specs=row(d),
        out_shape=jax.ShapeDtypeStruct((b, t, d), f32),
        compiler_params=_params("parallel", "parallel"),
        name="mix_cross",
    )(x, o_win, o_diff, y0, y1, g, bonus, ln_g.reshape(1, hw), ln_b.reshape(1, hw), wa, wb, wc,
      g_cross.reshape(1, d), wq, kv, wo)


def _ffn_up_kernel(x_ref, g_ref, wg_ref, wu_ref, o_ref):
    h = _rms(x_ref[...], g_ref[...]).astype(bf16)
    gate = jnp.dot(h, wg_ref[...], preferred_element_type=f32)
    up = jnp.dot(h, wu_ref[...], preferred_element_type=f32)
    o_ref[...] = (gate * _sigmoid(gate) * up).astype(o_ref.dtype)


def ffn_up(x, g, w_gu, layer, tm):
    m, d = x.shape
    return pl.pallas_call(
        _ffn_up_kernel,
        grid=(m // tm,),
        in_specs=[pl.BlockSpec((tm, d), lambda i: (i, 0)),
                  pl.BlockSpec((1, d), lambda i: (0, 0)),
                  pl.BlockSpec((None, d, D_FF), lambda i: (layer, 0, 0)),
                  pl.BlockSpec((None, d, D_FF), lambda i: (layer, 0, 1))],
        out_specs=pl.BlockSpec((tm, D_FF), lambda i: (i, 0)),
        out_shape=jax.ShapeDtypeStruct((m, D_FF), bf16),
        compiler_params=_params("parallel"),
        name="ffn_up",
    )(x, g.reshape(1, d), w_gu, w_gu)


def _ffn_down_kernel(x_ref, a_ref, w_ref, o_ref):
    o_ref[...] = x_ref[...] + jnp.dot(a_ref[...], w_ref[...], preferred_element_type=f32)


def _ffn_down_norm_kernel(x_ref, a_ref, w_ref, g_ref, o_ref):
    y = x_ref[...] + jnp.dot(a_ref[...], w_ref[...], preferred_element_type=f32)
    o_ref[...] = _rms(y, g_ref[...])


def ffn_down(x, act, w_down, layer, tm, final_gain=None):
    m, d = x.shape
    kf = act.shape[1]
    in_specs = [pl.BlockSpec((tm, d), lambda i: (i, 0)),
                pl.BlockSpec((tm, kf), lambda i: (i, 0)),
                pl.BlockSpec((None, kf, d), lambda i: (layer, 0, 0))]
    args = [x, act, w_down]
    if final_gain is not None:
        in_specs.append(pl.BlockSpec((1, d), lambda i: (0, 0)))
        args.append(final_gain.reshape(1, d))
    return pl.pallas_call(
        _ffn_down_kernel if final_gain is None else _ffn_down_norm_kernel,
        grid=(m // tm,),
        in_specs=in_specs,
        out_specs=pl.BlockSpec((tm, d), lambda i: (i, 0)),
        out_shape=jax.ShapeDtypeStruct((m, d), f32),
        compiler_params=_params("parallel"),
        name="ffn_down",
    )(*args)


def _row_tile(m, want):
    t = min(want, m)
    while m % t:
        t //= 2
    return t


def kernel(x, mem, norm_mix, w_in, win_sink, diff_lq1, diff_lk1, diff_lq2, diff_lk2, diff_subln,
           rwkv_conv, rwkv_w0, rwkv_w_up, rwkv_a0, rwkv_a_up, rwkv_g_up, rwkv_k_k, rwkv_k_a, rwkv_r_k,
           rwkv_ln_g, rwkv_ln_b, w_out, norm_cross, norm_mem, x_wq, x_wkv, x_wo, norm_ffn,
           ffn_w_gu, ffn_w_down, norm_final):
    b, t, d = x.shape
    nm = mem.shape[1]
    m = b * t
    tm = _row_tile(m, 1024)
    depth = w_in.shape[0]
    w_in_b, w_out_b, wq_b, wkv_b, wo_b, wgu_b, wdn_b = (
        z.astype(bf16) for z in (w_in, w_out, x_wq, x_wkv, x_wo, ffn_w_gu, ffn_w_down))
    mem2 = mem.reshape(b * nm, d)
    xf = x.reshape(m, d)
    for l in range(depth):
        qa, kvb = (z.reshape(b, t, IN_GROUP) for z in in_proj(xf, norm_mix[l], w_in_b, l, tm))
        o_win = window_attention(qa, win_sink[l], _row_tile(t, 256))
        lam_init = 0.8 - 0.6 * math.exp(-0.3 * l)
        o_diff = diff_attention(qa, kvb, diff_lq1[l], diff_lk1[l], diff_lq2[l], diff_lk2[l], diff_subln[l],
                                lam_init, _row_tile(t, 512))
        r, k, v, kkn, g, bonus, lw, a = rwkv_prep(
            xf.reshape(b, t, d), norm_mix[l], w_in_b, l, rwkv_conv[l], rwkv_w0[l], rwkv_w_up[l], rwkv_a0[l],
            rwkv_a_up[l], rwkv_g_up[l], rwkv_k_k[l], rwkv_k_a[l], rwkv_r_k[l], _row_tile(t, 512))
        y0, y1 = rwkv_scan(lw, a, k, kkn, v, r, rwkv_k_a[l], 4 if b % 4 == 0 else 1)
        kv = norm_matmul(mem2, norm_mem, wkv_b[l], _row_tile(b * nm, 512), 1024, bf16).reshape(b, nm, -1)
        xf = mix_cross(xf.reshape(b, t, d), o_win, o_diff, y0, y1, g, bonus, rwkv_ln_g[l], rwkv_ln_b[l],
                       w_out_b[l], kv, norm_cross[l], wq_b[l], wo_b[l], _row_tile(t, 512)).reshape(m, d)
        act = ffn_up(xf, norm_ffn[l], wgu_b, l, _row_tile(m, 512))
        xf = ffn_down(xf, act, wdn_b, l, tm, norm_final if l == depth - 1 else None)
    return xf.reshape(b, t, d)
```

```python
import functools
import math

import jax
import jax.numpy as jnp
from jax import lax
from jax.experimental import pallas as pl
from jax.experimental.pallas import tpu as pltpu

f32 = jnp.float32
bf16 = jnp.bfloat16

D_MODEL = 1024
DEPTH = 4
HEAD_DIM = 64
BLOCK = 128
WINDOW = 128
WIN_HEADS = 4
WIN_KV_HEADS = 2
DIFF_HEADS = 4
DIFF_QK_DIM = 64
DIFF_V_DIM = 2 * DIFF_QK_DIM
RWKV_HEADS = 4
RWKV_HEAD = 64
W_LORA = 64
A_LORA = 64
G_LORA = 128
X_HEADS = 4
X_HEAD_DIM = 128
D_FF = -(-8 * D_MODEL // (3 * 256)) * 256
RMS_EPS = 1e-6
GN_EPS = 64e-5

WIN_WIDTH = WIN_HEADS * HEAD_DIM
DIFF_WIDTH = DIFF_HEADS * DIFF_V_DIM
RWKV_WIDTH = RWKV_HEADS * RWKV_HEAD
RWKV_IN = 3 * RWKV_WIDTH + W_LORA + A_LORA + G_LORA
IN_WIDTH = 2 * WIN_WIDTH + 3 * DIFF_WIDTH + RWKV_IN

OFF_AQ = 0
OFF_AK = WIN_WIDTH
OFF_AV = OFF_AK + WIN_KV_HEADS * HEAD_DIM
OFF_BQ = OFF_AV + WIN_KV_HEADS * HEAD_DIM
OFF_BK = OFF_BQ + DIFF_WIDTH
OFF_BV = OFF_BK + DIFF_WIDTH
OFF_C = OFF_BV + DIFF_WIDTH
IN_GROUP = 1024
assert OFF_BK == IN_GROUP and OFF_C == 2 * IN_GROUP and IN_WIDTH == 3 * IN_GROUP and RWKV_IN == IN_GROUP

CHUNK = 64
INV_BLOCK = 8
assert INV_BLOCK == 8 and CHUNK == 8 * INV_BLOCK
WIDE = RWKV_HEADS * CHUNK
V7X_VMEM_LIMIT = 56 * 1024 * 1024
LOG2E = 1.4426950408889634
SOFTMAX_ROWS = 16

_SLOPES = [2.0 ** (-8.0 * i / (WIN_HEADS + DIFF_HEADS)) for i in range(1, WIN_HEADS + DIFF_HEADS + 1)]
SLOPES_WIN = _SLOPES[0::2]
SLOPES_DIFF = _SLOPES[1::2]


def _params(*sem):
    return pltpu.CompilerParams(dimension_semantics=sem, vmem_limit_bytes=V7X_VMEM_LIMIT)


def _dot(a, b):
    return jnp.dot(a.astype(bf16), b.astype(bf16), preferred_element_type=f32)


def _dot_nt(a, b):
    return lax.dot_general(a.astype(bf16), b.astype(bf16), (((1,), (1,)), ((), ())),
                           preferred_element_type=f32)


def _each(fn, *lists):
    return [fn(*args) for args in zip(*lists)]


def _split2(a):
    hi = a.astype(bf16)
    lo = (a - hi.astype(f32)).astype(bf16)
    return hi, lo


def _split3(a):
    hi = a.astype(bf16)
    r1 = a - hi.astype(f32)
    mid = r1.astype(bf16)
    lo = (r1 - mid.astype(f32)).astype(bf16)
    return hi, mid, lo


def _dot3(a, b, nt=False):
    d = _dot_nt if nt else _dot
    ah, al = _split2(a)
    bh, bl = _split2(b)
    return d(ah, bh) + (d(ah, bl) + d(al, bh))


def _dot_exact_rhs(a_bf16, b):
    b1, b2, b3 = _split3(b)
    return (jnp.dot(a_bf16, b1, preferred_element_type=f32)
            + (jnp.dot(a_bf16, b2, preferred_element_type=f32)
               + jnp.dot(a_bf16, b3, preferred_element_type=f32)))


def _group_sum(a, ones_bf16):
    a1, a2 = _split2(a)
    return (jnp.dot(a1, ones_bf16, preferred_element_type=f32)
            + jnp.dot(a2, ones_bf16, preferred_element_type=f32))


def _rms(x, g):
    ms = jnp.mean(x * x, axis=-1, keepdims=True)
    return (x * lax.rsqrt(ms + RMS_EPS)) * g


def _sigmoid(z):
    return 1.0 / (1.0 + jnp.exp(-z))


def _norm_matmul_kernel(x_ref, g_ref, w_ref, o_ref, h_ref):
    @pl.when(pl.program_id(1) == 0)
    def _():
        h_ref[...] = _rms(x_ref[...], g_ref[...]).astype(bf16)

    o_ref[...] = jnp.dot(h_ref[...], w_ref[...], preferred_element_type=f32).astype(o_ref.dtype)


def norm_matmul(x, g, w, tm, tn, out_dtype=f32):
    m, k = x.shape
    n = w.shape[1]
    return pl.pallas_call(
        _norm_matmul_kernel,
        grid=(m // tm, n // tn),
        in_specs=[pl.BlockSpec((tm, k), lambda i, j: (i, 0)),
                  pl.BlockSpec((1, k), lambda i, j: (0, 0)),
                  pl.BlockSpec((k, tn), lambda i, j: (0, j))],
        out_specs=pl.BlockSpec((tm, tn), lambda i, j: (i, j)),
        out_shape=jax.ShapeDtypeStruct((m, n), out_dtype),
        scratch_shapes=[pltpu.VMEM((tm, k), bf16)],
        compiler_params=_params("parallel", "arbitrary"),
        name="norm_matmul",
    )(x, g.reshape(1, k), w)


def _in_proj_kernel(x_ref, g_ref, w_ref, oa_ref, ob_ref, h_ref):
    j = pl.program_id(1)

    @pl.when(j == 0)
    def _():
        h_ref[...] = _rms(x_ref[...], g_ref[...]).astype(bf16)

    for idx, ref in enumerate((oa_ref, ob_ref)):
        @pl.when(j == idx)
        def _(ref=ref):
            ref[...] = jnp.dot(h_ref[...], w_ref[...], preferred_element_type=f32).astype(ref.dtype)


def in_proj(x, g, w, layer, tm):
    m, k = x.shape
    assert w.shape[2] == 3 * IN_GROUP
    out = pl.BlockSpec((tm, IN_GROUP), lambda i, j: (i, 0))
    return pl.pallas_call(
        _in_proj_kernel,
        grid=(m // tm, 2),
        in_specs=[pl.BlockSpec((tm, k), lambda i, j: (i, 0)),
                  pl.BlockSpec((1, k), lambda i, j: (0, 0)),
                  pl.BlockSpec((None, k, IN_GROUP), lambda i, j: (layer, 0, j))],
        out_specs=[out, out],
        out_shape=[jax.ShapeDtypeStruct((m, IN_GROUP), f32), jax.ShapeDtypeStruct((m, IN_GROUP), bf16)],
        scratch_shapes=[pltpu.VMEM((tm, k), bf16)],
        compiler_params=_params("parallel", "arbitrary"),
        name="in_proj",
    )(x, g.reshape(1, k), w)


def _win_kernel(sink_ref, q_ref, k_ref, v_ref, o_ref, *, seq, tq):
    i = pl.program_id(1)
    span = 3 * BLOCK
    group = WIN_HEADS // WIN_KV_HEADS
    q_all = (q_ref[0] * (HEAD_DIM ** -0.5 * LOG2E)).astype(bf16)
    chains = []
    qs, ks, vs, valids, dists = [], [], [], [], []
    for blk in range(tq // BLOCK):
        q0 = i * tq + blk * BLOCK
        start = pl.multiple_of(jnp.clip(q0 - BLOCK, 0, seq - span), BLOCK)
        kw = k_ref[0, pl.ds(start, span), :].astype(bf16)
        vw = v_ref[0, pl.ds(start, span), :].astype(bf16)
        qpos = q0 + lax.broadcasted_iota(jnp.int32, (BLOCK, span), 0)
        kpos = start + lax.broadcasted_iota(jnp.int32, (BLOCK, span), 1)
        dist = jnp.abs(kpos - qpos)
        valid = dist <= WINDOW
        distf = dist.astype(f32)
        for head in range(WIN_HEADS):
            kvh = head // group
            chains.append((blk, head))
            qs.append(q_all[blk * BLOCK:(blk + 1) * BLOCK, head * HEAD_DIM:(head + 1) * HEAD_DIM])
            ks.append(kw[:, kvh * HEAD_DIM:(kvh + 1) * HEAD_DIM])
            vs.append(vw[:, kvh * HEAD_DIM:(kvh + 1) * HEAD_DIM])
            valids.append(valid)
            dists.append(distf)
    sinks = [sink_ref[head] * LOG2E for _, head in chains]
    slopes = [SLOPES_WIN[head] * LOG2E for _, head in chains]
    s = _each(_dot_nt, qs, ks)
    s = _each(lambda x, ok, sl, dd: jnp.where(ok, x - sl * dd, -jnp.inf), s, valids, slopes, dists)
    m = _each(lambda x, sk: jnp.maximum(jnp.max(x, axis=-1, keepdims=True), sk), s, sinks)
    e = _each(lambda x, mm: jnp.exp2(x - mm), s, m)
    denom = _each(lambda x, mm, sk: jnp.sum(x, axis=-1, keepdims=True) + jnp.exp2(sk - mm), e, m, sinks)
    pv = _each(lambda x, vv: jnp.dot(x.astype(bf16), vv, preferred_element_type=f32), e, vs)
    outs = _each(lambda x, dd: x / dd, pv, denom)
    rows = [jnp.concatenate(outs[blk * WIN_HEADS:(blk + 1) * WIN_HEADS], axis=-1) for blk in range(tq // BLOCK)]
    o_ref[0] = jnp.concatenate(rows, axis=0).astype(o_ref.dtype)


def window_attention(proj, sink, tq):
    b, t, _ = proj.shape
    kvw = WIN_KV_HEADS * HEAD_DIM
    return pl.pallas_call(
        functools.partial(_win_kernel, seq=t, tq=tq),
        grid=(b, t // tq),
        in_specs=[pl.BlockSpec(memory_space=pltpu.SMEM),
                  pl.BlockSpec((1, tq, WIN_WIDTH), lambda bi, i: (bi, i, OFF_AQ // WIN_WIDTH)),
                  pl.BlockSpec((1, t, kvw), lambda bi, i: (bi, 0, OFF_AK // kvw)),
                  pl.BlockSpec((1, t, kvw), lambda bi, i: (bi, 0, OFF_AV // kvw))],
        out_specs=pl.BlockSpec((1, tq, WIN_WIDTH), lambda bi, i: (bi, i, 0)),
        out_shape=jax.ShapeDtypeStruct((b, t, WIN_WIDTH), bf16),
        compiler_params=_params("parallel", "arbitrary"),
        name="window_attention",
    )(sink, proj, proj, proj)


def _diff_kernel(slope_ref, lq1_ref, lk1_ref, lq2_ref, lk2_ref, g_ref, dist_ref, q_ref, k_ref, v_ref, o_ref,
                 s_ref, e_ref, vt_ref, mx_ref, *, seq, tq, lam_init):
    h = pl.program_id(1)
    nq = seq // tq
    nkb = seq // BLOCK
    groups = BLOCK // 8
    kt = 2 * BLOCK

    r = lax.broadcasted_iota(jnp.int32, (DIFF_V_DIM, DIFF_V_DIM), 0)
    c = lax.broadcasted_iota(jnp.int32, (DIFF_V_DIM, DIFF_V_DIM), 1)
    eye = jnp.where(r == c, 1.0, 0.0).astype(bf16)
    vt_ref[...] = _dot_nt(eye, v_ref[0]).astype(bf16)

    def step(i, do_a, do_b):
        slot_a = i % 2
        slot_b = 1 - slot_a
        if do_a:
            q = q_ref[0, i * tq:(i + 1) * tq, :] * (DIFF_QK_DIM ** -0.5 * LOG2E)
            lane = lax.broadcasted_iota(jnp.int32, q.shape, 1)
            qm = [jnp.where((lane // DIFF_QK_DIM) == c, q, 0.0).astype(bf16) for c in range(2)]
            first = (nq - 1 - i) * (tq // BLOCK)
            slope = slope_ref[h] * LOG2E
            mx = [None, None]
        if do_b:
            m = [jnp.max(mx_ref[slot_b, c], axis=0, keepdims=True) for c in range(2)]
            sm = [None, None]
            acc = [None, None]
        for kb in range(nkb):
            rows = slice(kb * BLOCK, (kb + 1) * BLOCK)
            if do_a:
                kblk = k_ref[0, rows, :]
                bias = slope * dist_ref[first + kb]
                for c in range(2):
                    s = _dot_nt(kblk, qm[c]) - bias
                    s_ref[slot_a, c, rows, :] = s
                    part = jnp.max(s.reshape(groups, 8, tq), axis=0)
                    mx[c] = part if mx[c] is None else jnp.maximum(mx[c], part)
            if do_b:
                for c in range(2):
                    e = jnp.exp2(s_ref[slot_b, c, rows, :] - m[c])
                    e_ref[c, rows, :] = e.astype(bf16)
                    part = jnp.sum(e.reshape(groups, 8, tq), axis=0)
                    sm[c] = part if sm[c] is None else sm[c] + part
                if (kb + 1) % (kt // BLOCK) == 0:
                    keys = slice((kb + 1) * BLOCK - kt, (kb + 1) * BLOCK)
                    for c in range(2):
                        part = jnp.dot(vt_ref[:, keys], e_ref[c, keys, :], preferred_element_type=f32)
                        acc[c] = part if acc[c] is None else acc[c] + part
        if do_a:
            for c in range(2):
                mx_ref[slot_a, c] = mx[c]
        if do_b:
            lam = (jnp.exp(jnp.sum(lq1_ref[...] * lk1_ref[...], axis=-1, keepdims=True))
                   - jnp.exp(jnp.sum(lq2_ref[...] * lk2_ref[...], axis=-1, keepdims=True)) + lam_init)
            outs = [acc[c] / jnp.sum(sm[c], axis=0, keepdims=True) for c in range(2)]
            ot = outs[0] - lam * outs[1]
            ms = jnp.mean(ot * ot, axis=0, keepdims=True)
            ot = ot * lax.rsqrt(ms + RMS_EPS) * g_ref[...] * (1.0 - lam_init)
            o_ref[0, (i - 1) * tq:i * tq, :] = ot.T.astype(o_ref.dtype)

    for i in range(nq + 1):
        pl.when(slope_ref[DIFF_HEADS] > 0.5)(functools.partial(step, i, i < nq, i > 0))


def _distance_table(seq, tq):
    off = (seq // tq - 1) * (tq // BLOCK)
    nj = seq // BLOCK + off
    j = lax.broadcasted_iota(jnp.int32, (nj, BLOCK, tq), 0)
    kl = lax.broadcasted_iota(jnp.int32, (nj, BLOCK, tq), 1)
    q = lax.broadcasted_iota(jnp.int32, (nj, BLOCK, tq), 2)
    return jnp.abs((j - off) * BLOCK + kl - q).astype(f32)


def diff_attention(qa, kv, lq1, lk1, lq2, lk2, subln, lam_init, tq):
    b, t, _ = qa.shape
    w = DIFF_V_DIM
    vec = pl.BlockSpec((1, DIFF_QK_DIM), lambda bi, h: (0, 0))
    slopes = jnp.asarray(SLOPES_DIFF + [1.0], f32)
    dist = _distance_table(t, tq)
    return pl.pallas_call(
        functools.partial(_diff_kernel, seq=t, tq=tq, lam_init=lam_init),
        grid=(b, DIFF_HEADS),
        in_specs=[pl.BlockSpec(memory_space=pltpu.SMEM), vec, vec, vec, vec,
                  pl.BlockSpec((w, 1), lambda bi, h: (0, 0)),
                  pl.BlockSpec(dist.shape, lambda bi, h: (0, 0, 0)),
                  pl.BlockSpec((1, t, w), lambda bi, h: (bi, 0, OFF_BQ // w + h)),
                  pl.BlockSpec((1, t, w), lambda bi, h: (bi, 0, (OFF_BK - IN_GROUP) // w + h)),
                  pl.BlockSpec((1, t, w), lambda bi, h: (bi, 0, (OFF_BV - IN_GROUP) // w + h))],
        out_specs=pl.BlockSpec((1, t, w), lambda bi, h: (bi, 0, h)),
        out_shape=jax.ShapeDtypeStruct((b, t, DIFF_WIDTH), bf16),
        scratch_shapes=[pltpu.VMEM((2, 2, t, tq), f32), pltpu.VMEM((2, t, tq), bf16), pltpu.VMEM((w, t), bf16),
                        pltpu.VMEM((2, 2, 8, tq), f32)],
        compiler_params=_params("parallel", "parallel"),
        name="diff_attention",
    )(slopes, lq1.reshape(1, -1), lk1.reshape(1, -1), lq2.reshape(1, -1), lk2.reshape(1, -1),
      subln.reshape(-1, 1), dist, qa, kv, kv)


def _head_ones():
    r = lax.broadcasted_iota(jnp.int32, (RWKV_WIDTH, RWKV_WIDTH), 0) // RWKV_HEAD
    c = lax.broadcasted_iota(jnp.int32, (RWKV_WIDTH, RWKV_WIDTH), 1) // RWKV_HEAD
    return jnp.where(r == c, 1.0, 0.0).astype(bf16)


def _rwkv_prep_kernel(x_ref, xp_ref, xn_ref, gmix_ref, wc_ref, conv_ref, w0_ref, wup_ref, a0_ref, aup_ref, gup_ref,
                      kk_ref, ka_ref, rk_ref,
                      r_out, k_out, v_out, kkn_out, g_out, bonus_out, lw_out, a_out, *, tt):
    i = pl.program_id(1)
    n = pl.num_programs(1)
    gmix = gmix_ref[...]
    h_prev = jnp.where(i > 0, _rms(xp_ref[0], gmix), 0.0)
    h_next = jnp.where(i < n - 1, _rms(xn_ref[0], gmix), 0.0)
    h = jnp.concatenate([h_prev, _rms(x_ref[0], gmix), h_next], axis=0).astype(bf16)
    proj = jnp.dot(h, wc_ref[...], preferred_element_type=f32)
    x = proj[8:tt + 8]
    first = proj[7:8]
    last = proj[tt + 8:tt + 9]
    c = pltpu.roll(x, 1, 0) * conv_ref[0:1, :] + x * conv_ref[1:2, :] + pltpu.roll(x, tt - 1, 0) * conv_ref[2:3, :]
    row8 = lax.broadcasted_iota(jnp.int32, (8, 1), 0)
    head = c[0:8] + jnp.where(row8 == 0, (first - x[tt - 1:tt]) * conv_ref[0:1, :], 0.0)
    tail = c[tt - 8:tt] + jnp.where(row8 == 7, (last - x[0:1]) * conv_ref[2:3, :], 0.0)
    c = jnp.concatenate([head, c[8:tt - 8], tail], axis=0)
    hw = RWKV_WIDTH
    r = c[:, 0:hw]
    k = c[:, hw:2 * hw]
    v = c[:, 2 * hw:3 * hw]
    o = 3 * hw
    wd = c[:, o:o + W_LORA]
    ad = c[:, o + W_LORA:o + W_LORA + A_LORA]
    gd = c[:, o + W_LORA + A_LORA:o + W_LORA + A_LORA + G_LORA]
    ones = _head_ones()
    kscaled = k * kk_ref[...]
    norm = jnp.sqrt(_group_sum(kscaled * kscaled, ones))
    kkn = kscaled / jnp.maximum(norm, 1e-12)
    tw = jnp.tanh(wd)
    rk_sum = None
    for d in range(2):
        z = w0_ref[d:d + 1, :] + _dot3(tw, wup_ref[d])
        lw_out[d, 0] = -math.exp(-0.5) * _sigmoid(z)
        a = _sigmoid(a0_ref[d:d + 1, :] + _dot(ad, aup_ref[d]))
        a_out[d, 0] = a
        kd = k * (1.0 + (a - 1.0) * ka_ref[...])
        rk = r * kd * rk_ref[...]
        rk_sum = rk if d == 0 else rk_sum + rk
    r_out[0] = r
    k_out[0] = k
    v_out[0] = v
    kkn_out[0] = kkn
    g_out[0] = _dot(_sigmoid(gd), gup_ref[...])
    bonus_out[0] = _group_sum(rk_sum, ones) * v


def rwkv_prep(x, g_mix, w_in, layer, conv, w0, w_up, a0, a_up, g_up, k_k, k_a, r_k, tt):
    b, t, d = x.shape
    hw = RWKV_WIDTH
    nblk8 = t // 8
    full = lambda shape: pl.BlockSpec(shape, lambda bi, i: (0,) * len(shape))
    tok = pl.BlockSpec((1, tt, hw), lambda bi, i: (bi, i, 0))
    tok2 = pl.BlockSpec((2, 1, tt, hw), lambda bi, i: (0, bi, i, 0))
    sds = jax.ShapeDtypeStruct((b, t, hw), f32)
    sds2 = jax.ShapeDtypeStruct((2, b, t, hw), f32)
    return pl.pallas_call(
        functools.partial(_rwkv_prep_kernel, tt=tt),
        grid=(b, t // tt),
        in_specs=[pl.BlockSpec((1, tt, d), lambda bi, i: (bi, i, 0)),
                  pl.BlockSpec((1, 8, d), lambda bi, i: (bi, jnp.maximum(i * (tt // 8) - 1, 0), 0)),
                  pl.BlockSpec((1, 8, d), lambda bi, i: (bi, jnp.minimum((i + 1) * (tt // 8), nblk8 - 1), 0)),
                  full((1, d)),
                  pl.BlockSpec((None, d, RWKV_IN), lambda bi, i: (layer, 0, OFF_C // RWKV_IN)),
                  full((3, RWKV_IN)), full((2, hw)), full((2, W_LORA, hw)), full((2, hw)),
                  full((2, A_LORA, hw)), full((G_LORA, hw)), full((1, hw)), full((1, hw)), full((1, hw))],
        out_specs=[tok, tok, tok, tok, tok, tok, tok2, tok2],
        out_shape=[sds, sds, sds, sds, sds, sds, sds2, sds2],
        compiler_params=_params("parallel", "arbitrary"),
        name="rwkv_prep",
    )(x, x, x, g_mix.reshape(1, d), w_in, conv, w0, w_up, a0, a_up, g_up,
      k_k.reshape(1, hw), k_a.reshape(1, hw), r_k.reshape(1, hw))


def _head_masks():
    lane_head = lax.broadcasted_iota(jnp.int32, (CHUNK, WIDE), 1) // RWKV_HEAD
    return [jnp.where(lane_head == h, 1.0, 0.0).astype(bf16) for h in range(RWKV_HEADS)]


def _stack(x, masks):
    xb = x.astype(bf16)
    return jnp.concatenate([xb * mk for mk in masks], axis=0)


def _mm(a, b, masks):
    return _dot(a, _stack(b, masks))


def _unit_triangular_inverse(ms, eye, masks):
    r = lax.broadcasted_iota(jnp.int32, eye.shape, 0) // INV_BLOCK
    c = (lax.broadcasted_iota(jnp.int32, eye.shape, 1) % CHUNK) // INV_BLOCK
    mm = lambda a, b: _mm(a, b, masks)
    d = [jnp.where(r == c, m, 0.0) for m in ms]
    off = _each(lambda m, dd: m - dd, ms, d)
    x = [-dd for dd in d]
    x2 = _each(lambda xx: mm(xx, xx), x)
    x4 = _each(lambda xx: mm(xx, xx), x2)
    p1 = _each(lambda u, w: mm(eye + u, eye + w), x, x2)
    td = _each(lambda u, w: mm(u, eye + w), p1, x4)
    z = _each(mm, td, off)
    z2 = _each(lambda zz: mm(zz, zz), z)
    z4 = _each(lambda zz: mm(zz, zz), z2)
    p3 = _each(lambda u, w: mm(eye - u, eye + w), z, z2)
    p4 = _each(lambda w, t: mm(eye + w, t), z4, td)
    return _each(mm, p3, p4)


def _rwkv_chunks(revs, lws, a_s, ks, kkns, vs, rs, k_a, h_refs):
    masks = _head_masks()
    ci = lax.broadcasted_iota(jnp.int32, (CHUNK, CHUNK), 0)
    cj = lax.broadcasted_iota(jnp.int32, (CHUNK, CHUNK), 1)
    tri_f = jnp.where(cj <= ci, 1.0, 0.0).astype(bf16)
    tri_r = jnp.where(cj >= ci, 1.0, 0.0).astype(bf16)
    ti = lax.broadcasted_iota(jnp.int32, (CHUNK, WIDE), 0)
    tj = lax.broadcasted_iota(jnp.int32, (CHUNK, WIDE), 1) % CHUNK
    eye = jnp.where(ti == tj, 1.0, 0.0)
    strict = [(tj > ti) if rev else (tj < ti) for rev in revs]
    incl = [(tj >= ti) if rev else (tj <= ti) for rev in revs]

    g_incl = [_dot_exact_rhs(tri_r if rev else tri_f, lw) for rev, lw in zip(revs, lws)]
    g_tot = [g[0:1, :] if rev else g[CHUNK - 1:CHUNK, :] for rev, g in zip(revs, g_incl)]
    kat = _each(lambda kkn, a: kkn * a, kkns, a_s)
    kd = _each(lambda k, a: k * (1.0 + (a - 1.0) * k_a), ks, a_s)
    e_neg = [jnp.exp(-g) for g in g_incl]
    e_rem = _each(lambda gt, g: jnp.exp(gt - g), g_tot, g_incl)
    a_t = _each(lambda kkn, g, lw: kkn * jnp.exp(g - lw), kkns, g_incl, lws)
    r_t = _each(lambda r, g: r * jnp.exp(g), rs, g_incl)
    b_t = _each(lambda x, e: x * e, kat, e_neg)
    k_t = _each(lambda x, e: x * e, kd, e_neg)
    b_hat = _each(lambda x, e: x * e, kat, e_rem)
    k_hat = _each(lambda x, e: x * e, kd, e_rem)

    ar = _each(lambda x, y: jnp.concatenate([x, y], axis=0).astype(bf16), a_t, r_t)
    s_b = _each(lambda x, y: _dot_nt(x, _stack(y, masks)), ar, b_t)
    s_k = _each(lambda x, y: _dot_nt(x, _stack(y, masks)), ar, k_t)
    m_ab = _each(lambda s, x: jnp.where(s, x[:CHUNK], 0.0), strict, s_b)
    a_rb = _each(lambda s, x: jnp.where(s, x[CHUNK:], 0.0), incl, s_b)
    m_ak = _each(lambda s, x: jnp.where(s, x[:CHUNK], 0.0), strict, s_k)
    a_rk = _each(lambda s, x: jnp.where(s, x[CHUNK:], 0.0), incl, s_k)
    t_inv = _unit_triangular_inverse(m_ab, eye, masks)

    wi = lax.broadcasted_iota(jnp.int32, (WIDE, WIDE), 0)
    wj = lax.broadcasted_iota(jnp.int32, (WIDE, WIDE), 1)
    same_head = (wi // RWKV_HEAD) == (wj // RWKV_HEAD)
    diag = wi == wj
    h = [h_ref[...].astype(bf16) for h_ref in h_refs]
    decay = [jnp.where(diag, jnp.exp(gt), 0.0).astype(bf16) for gt in g_tot]
    arh = _each(lambda x, dd, hh: _dot(jnp.concatenate([x, dd], axis=0), hh), ar, decay, h)
    v_s = _each(lambda x: _stack(x, masks), vs)
    mkv = _each(lambda m, n, vv: _dot(jnp.concatenate([m, n], axis=0), vv), m_ak, a_rk, v_s)
    u = _each(lambda t, x, y: -_mm(t, x[:CHUNK] + y[:CHUNK], masks), t_inv, arh, mkv)
    ys = _each(lambda x, m, uu, y: x[CHUNK:2 * CHUNK] + _mm(m, uu, masks) + y[CHUNK:], arh, a_rb, u, mkv)
    bk_t = _each(lambda x, y: jnp.concatenate([x, y], axis=0).T, b_hat, k_hat)
    for h_ref, x, bk, uu, vv in zip(h_refs, arh, bk_t, u, vs):
        upd = _dot(bk, jnp.concatenate([uu, vv], axis=0))
        h_ref[...] = x[2 * CHUNK:] + jnp.where(same_head, upd, 0.0)
    return ys


def _rwkv_scan_kernel(ka_ref,
                      lw0, a0, k0, kk0, v0, r0,
                      lw1, a1, k1, kk1, v1, r1,
                      y0_ref, y1_ref, h_ref, *, rows):
    @pl.when(pl.program_id(1) == 0)
    def _():
        h_ref[...] = jnp.zeros_like(h_ref)

    revs = [False] * rows + [True] * rows
    gather = lambda f, b: [f[0, g] for g in range(rows)] + [b[0, g] for g in range(rows)]
    plain = lambda f, b: [f[g] for g in range(rows)] + [b[g] for g in range(rows)]
    ys = _rwkv_chunks(revs, gather(lw0, lw1), gather(a0, a1), plain(k0, k1), plain(kk0, kk1),
                      plain(v0, v1), plain(r0, r1), ka_ref[...], [h_ref.at[i] for i in range(2 * rows)])
    for g in range(rows):
        y0_ref[g] = ys[g]
        y1_ref[g] = ys[rows + g]


def rwkv_scan(lw, a, k, kkn, v, r, k_a, rows):
    b, t, hw = k.shape
    nc = t // CHUNK
    fwd = pl.BlockSpec((rows, CHUNK, hw), lambda bi, c: (bi, c, 0))
    bwd = pl.BlockSpec((rows, CHUNK, hw), lambda bi, c: (bi, nc - 1 - c, 0))
    fwd2 = pl.BlockSpec((1, rows, CHUNK, hw), lambda bi, c: (0, bi, c, 0))
    bwd2 = pl.BlockSpec((1, rows, CHUNK, hw), lambda bi, c: (1, bi, nc - 1 - c, 0))
    sds = jax.ShapeDtypeStruct((b, t, hw), f32)
    return pl.pallas_call(
        functools.partial(_rwkv_scan_kernel, rows=rows),
        grid=(b // rows, nc),
        in_specs=[pl.BlockSpec((1, hw), lambda bi, c: (0, 0)),
                  fwd2, fwd2, fwd, fwd, fwd, fwd,
                  bwd2, bwd2, bwd, bwd, bwd, bwd],
        out_specs=[fwd, bwd],
        out_shape=[sds, sds],
        scratch_shapes=[pltpu.VMEM((2 * rows, WIDE, WIDE), f32)],
        compiler_params=_params("parallel", "arbitrary"),
        name="rwkv_scan",
    )(k_a.reshape(1, hw), lw, a, k, kkn, v, r, lw, a, k, kkn, v, r)


def _rwkv_finish(y0, y1, g, bonus, ln_g, ln_b):
    ones = _head_ones()
    y = y0 + y1
    mu = _group_sum(y, ones) * (1.0 / RWKV_HEAD)
    yc = y - mu
    var = _group_sum(yc * yc, ones) * (1.0 / RWKV_HEAD)
    yn = yc * lax.rsqrt(var + GN_EPS) * ln_g + ln_b
    return (yn + bonus) * g


def _mix_cross_kernel(x_ref, a_ref, b_ref, y0_ref, y1_ref, g_ref, bonus_ref, lng_ref, lnb_ref,
                      wa_ref, wb_ref, wc_ref, gx_ref, wq_ref, kv_ref, wo_ref, o_ref):
    c = _rwkv_finish(y0_ref[0], y1_ref[0], g_ref[0], bonus_ref[0], lng_ref[...], lnb_ref[...])
    acc = jnp.dot(a_ref[0], wa_ref[...], preferred_element_type=f32)
    acc = acc + jnp.dot(b_ref[0], wb_ref[...], preferred_element_type=f32)
    acc = acc + jnp.dot(c.astype(bf16), wc_ref[...], preferred_element_type=f32)
    x = x_ref[0] + acc
    q = jnp.dot(_rms(x, gx_ref[...]).astype(bf16), wq_ref[...], preferred_element_type=f32)
    q = (q * (X_HEAD_DIM ** -0.5 * LOG2E)).astype(bf16)
    kv = kv_ref[0]
    xw = X_HEADS * X_HEAD_DIM
    heads = range(X_HEADS)
    s = [_dot_nt(q[:, h * X_HEAD_DIM:(h + 1) * X_HEAD_DIM], kv[:, h * X_HEAD_DIM:(h + 1) * X_HEAD_DIM])
         for h in heads]
    e = _each(lambda z: jnp.exp2(z - jnp.max(z, axis=-1, keepdims=True)), s)
    denom = _each(lambda z: jnp.sum(z, axis=-1, keepdims=True), e)
    pv = [jnp.dot(z.astype(bf16), kv[:, xw + h * X_HEAD_DIM:xw + (h + 1) * X_HEAD_DIM],
                  preferred_element_type=f32) for h, z in zip(heads, e)]
    o = jnp.concatenate(_each(lambda z, dd: z / dd, pv, denom), axis=-1)
    o_ref[0] = x + jnp.dot(o.astype(bf16), wo_ref[...], preferred_element_type=f32)


def mix_cross(x, o_win, o_diff, y0, y1, g, bonus, ln_g, ln_b, w_out, kv, g_cross, wq, wo, tm):
    b, t, d = x.shape
    nm = kv.shape[1]
    hw = RWKV_WIDTH
    xw = X_HEADS * X_HEAD_DIM
    wa = w_out[:WIN_WIDTH]
    wb = w_out[WIN_WIDTH:WIN_WIDTH + DIFF_WIDTH]
    wc = w_out[WIN_WIDTH + DIFF_WIDTH:]
    row = lambda w: pl.BlockSpec((1, tm, w), lambda bi, i: (bi, i, 0))
    full = lambda r, c: pl.BlockSpec((r, c), lambda bi, i: (0, 0))
    return pl.pallas_call(
        _mix_cross_kernel,
        grid=(b, t // tm),
        in_specs=[row(d), row(WIN_WIDTH), row(DIFF_WIDTH), row(hw), row(hw), row(hw), row(hw),
                  full(1, hw), full(1, hw), full(WIN_WIDTH, d), full(DIFF_WIDTH, d), full(hw, d),
                  full(1, d), full(d, xw),
                  pl.BlockSpec((1, nm, 2 * xw), lambda bi, i: (bi, 0, 0)),
                  full(xw, d)],
        out_specs=row(d),
        out_shape=jax.ShapeDtypeStruct((b, t, d), f32),
        compiler_params=_params("parallel", "parallel"),
        name="mix_cross",
    )(x, o_win, o_diff, y0, y1, g, bonus, ln_g.reshape(1, hw), ln_b.reshape(1, hw), wa, wb, wc,
      g_cross.reshape(1, d), wq, kv, wo)


def _ffn_up_kernel(x_ref, g_ref, wg_ref, wu_ref, o_ref):
    h = _rms(x_ref[...], g_ref[...]).astype(bf16)
    gate = jnp.dot(h, wg_ref[...], preferred_element_type=f32)
    up = jnp.dot(h, wu_ref[...], preferred_element_type=f32)
    o_ref[...] = (gate * _sigmoid(gate) * up).astype(o_ref.dtype)


def ffn_up(x, g, w_gu, layer, tm):
    m, d = x.shape
    return pl.pallas_call(
        _ffn_up_kernel,
        grid=(m // tm,),
        in_specs=[pl.BlockSpec((tm, d), lambda i: (i, 0)),
                  pl.BlockSpec((1, d), lambda i: (0, 0)),
                  pl.BlockSpec((None, d, D_FF), lambda i: (layer, 0, 0)),
                  pl.BlockSpec((None, d, D_FF), lambda i: (layer, 0, 1))],
        out_specs=pl.BlockSpec((tm, D_FF), lambda i: (i, 0)),
        out_shape=jax.ShapeDtypeStruct((m, D_FF), bf16),
        compiler_params=_params("parallel"),
        name="ffn_up",
    )(x, g.reshape(1, d), w_gu, w_gu)


def _ffn_down_kernel(x_ref, a_ref, w_ref, o_ref):
    o_ref[...] = x_ref[...] + jnp.dot(a_ref[...], w_ref[...], preferred_element_type=f32)


def _ffn_down_norm_kernel(x_ref, a_ref, w_ref, g_ref, o_ref):
    y = x_ref[...] + jnp.dot(a_ref[...], w_ref[...], preferred_element_type=f32)
    o_ref[...] = _rms(y, g_ref[...])


def ffn_down(x, act, w_down, layer, tm, final_gain=None):
    m, d = x.shape
    kf = act.shape[1]
    in_specs = [pl.BlockSpec((tm, d), lambda i: (i, 0)),
                pl.BlockSpec((tm, kf), lambda i: (i, 0)),
                pl.BlockSpec((None, kf, d), lambda i: (layer, 0, 0))]
    args = [x, act, w_down]
    if final_gain is not None:
        in_specs.append(pl.BlockSpec((1, d), lambda i: (0, 0)))
        args.append(final_gain.reshape(1, d))
    return pl.pallas_call(
        _ffn_down_kernel if final_gain is None else _ffn_down_norm_kernel,
        grid=(m // tm,),
        in_specs=in_specs,
        out_specs=pl.BlockSpec((tm, d), lambda i: (i, 0)),
        out_shape=jax.ShapeDtypeStruct((m, d), f32),
        compiler_params=_params("parallel"),
        name="ffn_down",
    )(*args)


def _row_tile(m, want):
    t = min(want, m)
    while m % t:
        t //= 2
    return t


def kernel(x, mem, norm_mix, w_in, win_sink, diff_lq1, diff_lk1, diff_lq2, diff_lk2, diff_subln,
           rwkv_conv, rwkv_w0, rwkv_w_up, rwkv_a0, rwkv_a_up, rwkv_g_up, rwkv_k_k, rwkv_k_a, rwkv_r_k,
           rwkv_ln_g, rwkv_ln_b, w_out, norm_cross, norm_mem, x_wq, x_wkv, x_wo, norm_ffn,
           ffn_w_gu, ffn_w_down, norm_final):
    b, t, d = x.shape
    nm = mem.shape[1]
    m = b * t
    tm = _row_tile(m, 1024)
    depth = w_in.shape[0]
    w_in_b, w_out_b, wq_b, wkv_b, wo_b, wgu_b, wdn_b = (
        z.astype(bf16) for z in (w_in, w_out, x_wq, x_wkv, x_wo, ffn_w_gu, ffn_w_down))
    mem2 = mem.reshape(b * nm, d)
    xf = x.reshape(m, d)
    for l in range(depth):
        qa, kvb = (z.reshape(b, t, IN_GROUP) for z in in_proj(xf, norm_mix[l], w_in_b, l, tm))
        o_win = window_attention(qa, win_sink[l], _row_tile(t, 512))
        lam_init = 0.8 - 0.6 * math.exp(-0.3 * l)
        o_diff = diff_attention(qa, kvb, diff_lq1[l], diff_lk1[l], diff_lq2[l], diff_lk2[l], diff_subln[l],
                                lam_init, _row_tile(t, 512))
        r, k, v, kkn, g, bonus, lw, a = rwkv_prep(
            xf.reshape(b, t, d), norm_mix[l], w_in_b, l, rwkv_conv[l], rwkv_w0[l], rwkv_w_up[l], rwkv_a0[l],
            rwkv_a_up[l], rwkv_g_up[l], rwkv_k_k[l], rwkv_k_a[l], rwkv_r_k[l], _row_tile(t, 512))
        y0, y1 = rwkv_scan(lw, a, k, kkn, v, r, rwkv_k_a[l], 8 if b % 8 == 0 else 1)
        kv = norm_matmul(mem2, norm_mem, wkv_b[l], _row_tile(b * nm, 512), 1024, bf16).reshape(b, nm, -1)
        xf = mix_cross(xf.reshape(b, t, d), o_win, o_diff, y0, y1, g, bonus, rwkv_ln_g[l], rwkv_ln_b[l],
                       w_out_b[l], kv, norm_cross[l], wq_b[l], wo_b[l], _row_tile(t, 512)).reshape(m, d)
        act = ffn_up(xf, norm_ffn[l], wgu_b, l, _row_tile(m, 512))
        xf = ffn_down(xf, act, wdn_b, l, tm, norm_final if l == depth - 1 else None)
    return xf.reshape(b, t, d)
```

```python
import functools
import math
from typing import NamedTuple

import jax
import jax.numpy as jnp
from jax import lax
from jax.experimental import pallas as pl
from jax.experimental.pallas import tpu as pltpu

f32 = jnp.float32
bf16 = jnp.bfloat16

D_MODEL = 1024
DEPTH = 4
HEAD_DIM = 64
BLOCK = 128
WINDOW = 128
WIN_HEADS = 4
WIN_KV_HEADS = 2
DIFF_HEADS = 4
DIFF_QK_DIM = 64
DIFF_V_DIM = 2 * DIFF_QK_DIM
RWKV_HEADS = 4
RWKV_HEAD = 64
W_LORA = 64
A_LORA = 64
G_LORA = 128
X_HEADS = 4
X_HEAD_DIM = 128
D_FF = -(-8 * D_MODEL // (3 * 256)) * 256
RMS_EPS = 1e-6
GN_EPS = 64e-5

WIN_WIDTH = WIN_HEADS * HEAD_DIM
DIFF_WIDTH = DIFF_HEADS * DIFF_V_DIM
RWKV_WIDTH = RWKV_HEADS * RWKV_HEAD
RWKV_IN = 3 * RWKV_WIDTH + W_LORA + A_LORA + G_LORA
IN_WIDTH = 2 * WIN_WIDTH + 3 * DIFF_WIDTH + RWKV_IN

OFF_AQ = 0
OFF_AK = WIN_WIDTH
OFF_AV = OFF_AK + WIN_KV_HEADS * HEAD_DIM
OFF_BQ = OFF_AV + WIN_KV_HEADS * HEAD_DIM
OFF_BK = OFF_BQ + DIFF_WIDTH
OFF_BV = OFF_BK + DIFF_WIDTH
OFF_C = OFF_BV + DIFF_WIDTH
IN_GROUP = 1024
assert OFF_BK == IN_GROUP and OFF_C == 2 * IN_GROUP and IN_WIDTH == 3 * IN_GROUP and RWKV_IN == IN_GROUP

CHUNK = 64
INV_BLOCK = 8
assert INV_BLOCK == 8 and CHUNK == 8 * INV_BLOCK
WIDE = RWKV_HEADS * CHUNK
V7X_VMEM_LIMIT = 56 * 1024 * 1024
LOG2E = 1.4426950408889634

_SLOPES = [2.0 ** (-8.0 * i / (WIN_HEADS + DIFF_HEADS)) for i in range(1, WIN_HEADS + DIFF_HEADS + 1)]
SLOPES_WIN = _SLOPES[0::2]
SLOPES_DIFF = _SLOPES[1::2]


def _params(*sem):
    return pltpu.CompilerParams(dimension_semantics=sem, vmem_limit_bytes=V7X_VMEM_LIMIT)


def _dot(a, b):
    return jnp.dot(a.astype(bf16), b.astype(bf16), preferred_element_type=f32)


def _dot_nt(a, b):
    return lax.dot_general(a.astype(bf16), b.astype(bf16), (((1,), (1,)), ((), ())),
                           preferred_element_type=f32)


def _each(fn, *lists):
    return [fn(*args) for args in zip(*lists)]


def _split2(a):
    hi = a.astype(bf16)
    lo = (a - hi.astype(f32)).astype(bf16)
    return hi, lo


def _split3(a):
    hi = a.astype(bf16)
    r1 = a - hi.astype(f32)
    mid = r1.astype(bf16)
    lo = (r1 - mid.astype(f32)).astype(bf16)
    return hi, mid, lo


def _dot3(a, b, nt=False):
    d = _dot_nt if nt else _dot
    ah, al = _split2(a)
    bh, bl = _split2(b)
    return d(ah, bh) + (d(ah, bl) + d(al, bh))


def _dot_exact_rhs(a_bf16, b):
    b1, b2, b3 = _split3(b)
    return (jnp.dot(a_bf16, b1, preferred_element_type=f32)
            + (jnp.dot(a_bf16, b2, preferred_element_type=f32)
               + jnp.dot(a_bf16, b3, preferred_element_type=f32)))


def _group_sum(a, ones_bf16):
    a1, a2 = _split2(a)
    return (jnp.dot(a1, ones_bf16, preferred_element_type=f32)
            + jnp.dot(a2, ones_bf16, preferred_element_type=f32))


def _rms(x, g):
    ms = jnp.mean(x * x, axis=-1, keepdims=True)
    return (x * lax.rsqrt(ms + RMS_EPS)) * g


def _sigmoid(z):
    return 1.0 / (1.0 + jnp.exp(-z))


def _norm_matmul_kernel(x_ref, g_ref, w_ref, o_ref, h_ref):
    @pl.when(pl.program_id(1) == 0)
    def _():
        h_ref[...] = _rms(x_ref[...], g_ref[...]).astype(bf16)

    o_ref[...] = jnp.dot(h_ref[...], w_ref[...], preferred_element_type=f32).astype(o_ref.dtype)


def norm_matmul(x, g, w, tm, tn, out_dtype=f32):
    m, k = x.shape
    n = w.shape[1]
    return pl.pallas_call(
        _norm_matmul_kernel,
        grid=(m // tm, n // tn),
        in_specs=[pl.BlockSpec((tm, k), lambda i, j: (i, 0)),
                  pl.BlockSpec((1, k), lambda i, j: (0, 0)),
                  pl.BlockSpec((k, tn), lambda i, j: (0, j))],
        out_specs=pl.BlockSpec((tm, tn), lambda i, j: (i, j)),
        out_shape=jax.ShapeDtypeStruct((m, n), out_dtype),
        scratch_shapes=[pltpu.VMEM((tm, k), bf16)],
        compiler_params=_params("parallel", "arbitrary"),
        name="norm_matmul",
    )(x, g.reshape(1, k), w)


def _in_proj_kernel(x_ref, g_ref, w_ref, oa_ref, ob_ref, h_ref):
    j = pl.program_id(1)

    @pl.when(j == 0)
    def _():
        h_ref[...] = _rms(x_ref[...], g_ref[...]).astype(bf16)

    for idx, ref in enumerate((oa_ref, ob_ref)):
        @pl.when(j == idx)
        def _(ref=ref):
            ref[...] = jnp.dot(h_ref[...], w_ref[...], preferred_element_type=f32).astype(ref.dtype)


def in_proj(x, g, w, layer, tm):
    m, k = x.shape
    assert w.shape[2] == 3 * IN_GROUP
    out = pl.BlockSpec((tm, IN_GROUP), lambda i, j: (i, 0))
    return pl.pallas_call(
        _in_proj_kernel,
        grid=(m // tm, 2),
        in_specs=[pl.BlockSpec((tm, k), lambda i, j: (i, 0)),
                  pl.BlockSpec((1, k), lambda i, j: (0, 0)),
                  pl.BlockSpec((None, k, IN_GROUP), lambda i, j: (layer, 0, j))],
        out_specs=[out, out],
        out_shape=[jax.ShapeDtypeStruct((m, IN_GROUP), f32), jax.ShapeDtypeStruct((m, IN_GROUP), bf16)],
        scratch_shapes=[pltpu.VMEM((tm, k), bf16)],
        compiler_params=_params("parallel", "arbitrary"),
        name="in_proj",
    )(x, g.reshape(1, k), w)


def _win_kernel(sink_ref, q_ref, k_ref, v_ref, o_ref, *, seq, tq):
    i = pl.program_id(1)
    span = 3 * BLOCK
    group = WIN_HEADS // WIN_KV_HEADS
    q_all = (q_ref[0] * (HEAD_DIM ** -0.5 * LOG2E)).astype(bf16)
    chains = []
    qs, ks, vs, valids, dists = [], [], [], [], []
    for blk in range(tq // BLOCK):
        q0 = i * tq + blk * BLOCK
        start = pl.multiple_of(jnp.clip(q0 - BLOCK, 0, seq - span), BLOCK)
        kw = k_ref[0, pl.ds(start, span), :].astype(bf16)
        vw = v_ref[0, pl.ds(start, span), :].astype(bf16)
        qpos = q0 + lax.broadcasted_iota(jnp.int32, (BLOCK, span), 0)
        kpos = start + lax.broadcasted_iota(jnp.int32, (BLOCK, span), 1)
        dist = jnp.abs(kpos - qpos)
        valid = dist <= WINDOW
        distf = dist.astype(f32)
        for head in range(WIN_HEADS):
            kvh = head // group
            chains.append((blk, head))
            qs.append(q_all[blk * BLOCK:(blk + 1) * BLOCK, head * HEAD_DIM:(head + 1) * HEAD_DIM])
            ks.append(kw[:, kvh * HEAD_DIM:(kvh + 1) * HEAD_DIM])
            vs.append(vw[:, kvh * HEAD_DIM:(kvh + 1) * HEAD_DIM])
            valids.append(valid)
            dists.append(distf)
    sinks = [sink_ref[head] * LOG2E for _, head in chains]
    slopes = [SLOPES_WIN[head] * LOG2E for _, head in chains]
    s = _each(_dot_nt, qs, ks)
    s = _each(lambda x, ok, sl, dd: jnp.where(ok, x - sl * dd, -jnp.inf), s, valids, slopes, dists)
    m = _each(lambda x, sk: jnp.maximum(jnp.max(x, axis=-1, keepdims=True), sk), s, sinks)
    e = _each(lambda x, mm: jnp.exp2(x - mm), s, m)
    denom = _each(lambda x, mm, sk: jnp.sum(x, axis=-1, keepdims=True) + jnp.exp2(sk - mm), e, m, sinks)
    pv = _each(lambda x, vv: jnp.dot(x.astype(bf16), vv, preferred_element_type=f32), e, vs)
    outs = _each(lambda x, dd: x / dd, pv, denom)
    rows = [jnp.concatenate(outs[blk * WIN_HEADS:(blk + 1) * WIN_HEADS], axis=-1) for blk in range(tq // BLOCK)]
    o_ref[0] = jnp.concatenate(rows, axis=0).astype(o_ref.dtype)


def window_attention(proj, sink, tq):
    b, t, _ = proj.shape
    kvw = WIN_KV_HEADS * HEAD_DIM
    return pl.pallas_call(
        functools.partial(_win_kernel, seq=t, tq=tq),
        grid=(b, t // tq),
        in_specs=[pl.BlockSpec(memory_space=pltpu.SMEM),
                  pl.BlockSpec((1, tq, WIN_WIDTH), lambda bi, i: (bi, i, OFF_AQ // WIN_WIDTH)),
                  pl.BlockSpec((1, t, kvw), lambda bi, i: (bi, 0, OFF_AK // kvw)),
                  pl.BlockSpec((1, t, kvw), lambda bi, i: (bi, 0, OFF_AV // kvw))],
        out_specs=pl.BlockSpec((1, tq, WIN_WIDTH), lambda bi, i: (bi, i, 0)),
        out_shape=jax.ShapeDtypeStruct((b, t, WIN_WIDTH), bf16),
        compiler_params=_params("parallel", "arbitrary"),
        name="window_attention",
    )(sink, proj, proj, proj)


def _diff_kernel(slope_ref, lq1_ref, lk1_ref, lq2_ref, lk2_ref, g_ref, dist_ref, q_ref, k_ref, v_ref, o_ref,
                 s_ref, e_ref, vt_ref, mx_ref, *, seq, tq, lam_init):
    h = pl.program_id(1)
    nq = seq // tq
    nkb = seq // BLOCK
    groups = BLOCK // 8
    kt = 2 * BLOCK

    r = lax.broadcasted_iota(jnp.int32, (DIFF_V_DIM, DIFF_V_DIM), 0)
    c = lax.broadcasted_iota(jnp.int32, (DIFF_V_DIM, DIFF_V_DIM), 1)
    eye = jnp.where(r == c, 1.0, 0.0).astype(bf16)
    vt_ref[...] = _dot_nt(eye, v_ref[0]).astype(bf16)

    def step(i, do_a, do_b):
        slot_a = i % 2
        slot_b = 1 - slot_a
        if do_a:
            q = q_ref[0, i * tq:(i + 1) * tq, :] * (DIFF_QK_DIM ** -0.5 * LOG2E)
            lane = lax.broadcasted_iota(jnp.int32, q.shape, 1)
            qm = [jnp.where((lane // DIFF_QK_DIM) == c, q, 0.0).astype(bf16) for c in range(2)]
            first = (nq - 1 - i) * (tq // BLOCK)
            slope = slope_ref[h] * LOG2E
            mx = [None, None]
        if do_b:
            m = [jnp.max(mx_ref[slot_b, c], axis=0, keepdims=True) for c in range(2)]
            sm = [None, None]
            acc = [None, None]
        for kb in range(nkb):
            rows = slice(kb * BLOCK, (kb + 1) * BLOCK)
            if do_a:
                kblk = k_ref[0, rows, :]
                bias = slope * dist_ref[first + kb]
                for c in range(2):
                    s = _dot_nt(kblk, qm[c]) - bias
                    s_ref[slot_a, c, rows, :] = s
                    part = jnp.max(s.reshape(groups, 8, tq), axis=0)
                    mx[c] = part if mx[c] is None else jnp.maximum(mx[c], part)
            if do_b:
                for c in range(2):
                    e = jnp.exp2(s_ref[slot_b, c, rows, :] - m[c])
                    e_ref[c, rows, :] = e.astype(bf16)
                    part = jnp.sum(e.reshape(groups, 8, tq), axis=0)
                    sm[c] = part if sm[c] is None else sm[c] + part
                if (kb + 1) % (kt // BLOCK) == 0:
                    keys = slice((kb + 1) * BLOCK - kt, (kb + 1) * BLOCK)
                    for c in range(2):
                        part = jnp.dot(vt_ref[:, keys], e_ref[c, keys, :], preferred_element_type=f32)
                        acc[c] = part if acc[c] is None else acc[c] + part
        if do_a:
            for c in range(2):
                mx_ref[slot_a, c] = mx[c]
        if do_b:
            lam = (jnp.exp(jnp.sum(lq1_ref[...] * lk1_ref[...], axis=-1, keepdims=True))
                   - jnp.exp(jnp.sum(lq2_ref[...] * lk2_ref[...], axis=-1, keepdims=True)) + lam_init)
            outs = [acc[c] / jnp.sum(sm[c], axis=0, keepdims=True) for c in range(2)]
            ot = outs[0] - lam * outs[1]
            ms = jnp.mean(ot * ot, axis=0, keepdims=True)
            ot = ot * lax.rsqrt(ms + RMS_EPS) * g_ref[...] * (1.0 - lam_init)
            o_ref[0, (i - 1) * tq:i * tq, :] = ot.T.astype(o_ref.dtype)

    for i in range(nq + 1):
        pl.when(slope_ref[DIFF_HEADS] > 0.5)(functools.partial(step, i, i < nq, i > 0))


def _distance_table(seq, tq):
    off = (seq // tq - 1) * (tq // BLOCK)
    nj = seq // BLOCK + off
    j = lax.broadcasted_iota(jnp.int32, (nj, BLOCK, tq), 0)
    kl = lax.broadcasted_iota(jnp.int32, (nj, BLOCK, tq), 1)
    q = lax.broadcasted_iota(jnp.int32, (nj, BLOCK, tq), 2)
    return jnp.abs((j - off) * BLOCK + kl - q).astype(f32)


def diff_attention(qa, kv, lq1, lk1, lq2, lk2, subln, lam_init, tq):
    b, t, _ = qa.shape
    w = DIFF_V_DIM
    vec = pl.BlockSpec((1, DIFF_QK_DIM), lambda bi, h: (0, 0))
    slopes = jnp.asarray(SLOPES_DIFF + [1.0], f32)
    dist = _distance_table(t, tq)
    return pl.pallas_call(
        functools.partial(_diff_kernel, seq=t, tq=tq, lam_init=lam_init),
        grid=(b, DIFF_HEADS),
        in_specs=[pl.BlockSpec(memory_space=pltpu.SMEM), vec, vec, vec, vec,
                  pl.BlockSpec((w, 1), lambda bi, h: (0, 0)),
                  pl.BlockSpec(dist.shape, lambda bi, h: (0, 0, 0)),
                  pl.BlockSpec((1, t, w), lambda bi, h: (bi, 0, OFF_BQ // w + h)),
                  pl.BlockSpec((1, t, w), lambda bi, h: (bi, 0, (OFF_BK - IN_GROUP) // w + h)),
                  pl.BlockSpec((1, t, w), lambda bi, h: (bi, 0, (OFF_BV - IN_GROUP) // w + h))],
        out_specs=pl.BlockSpec((1, t, w), lambda bi, h: (bi, 0, h)),
        out_shape=jax.ShapeDtypeStruct((b, t, DIFF_WIDTH), bf16),
        scratch_shapes=[pltpu.VMEM((2, 2, t, tq), f32), pltpu.VMEM((2, t, tq), bf16), pltpu.VMEM((w, t), bf16),
                        pltpu.VMEM((2, 2, 8, tq), f32)],
        compiler_params=_params("parallel", "parallel"),
        name="diff_attention",
    )(slopes, lq1.reshape(1, -1), lk1.reshape(1, -1), lq2.reshape(1, -1), lk2.reshape(1, -1),
      subln.reshape(-1, 1), dist, qa, kv, kv)


def _head_ones():
    r = lax.broadcasted_iota(jnp.int32, (RWKV_WIDTH, RWKV_WIDTH), 0) // RWKV_HEAD
    c = lax.broadcasted_iota(jnp.int32, (RWKV_WIDTH, RWKV_WIDTH), 1) // RWKV_HEAD
    return jnp.where(r == c, 1.0, 0.0).astype(bf16)


def _rwkv_prep_kernel(x_ref, xp_ref, xn_ref, gmix_ref, wc_ref, conv_ref, w0_ref, wup_ref, a0_ref, aup_ref, gup_ref,
                      kk_ref, ka_ref, rk_ref,
                      r_out, k_out, v_out, kkn_out, g_out, bonus_out, lw_out, a_out, *, tt):
    i = pl.program_id(1)
    n = pl.num_programs(1)
    gmix = gmix_ref[...]
    h_prev = jnp.where(i > 0, _rms(xp_ref[0], gmix), 0.0)
    h_next = jnp.where(i < n - 1, _rms(xn_ref[0], gmix), 0.0)
    h = jnp.concatenate([h_prev, _rms(x_ref[0], gmix), h_next], axis=0).astype(bf16)
    proj = jnp.dot(h, wc_ref[...], preferred_element_type=f32)
    x = proj[8:tt + 8]
    first = proj[7:8]
    last = proj[tt + 8:tt + 9]
    c = pltpu.roll(x, 1, 0) * conv_ref[0:1, :] + x * conv_ref[1:2, :] + pltpu.roll(x, tt - 1, 0) * conv_ref[2:3, :]
    row8 = lax.broadcasted_iota(jnp.int32, (8, 1), 0)
    head = c[0:8] + jnp.where(row8 == 0, (first - x[tt - 1:tt]) * conv_ref[0:1, :], 0.0)
    tail = c[tt - 8:tt] + jnp.where(row8 == 7, (last - x[0:1]) * conv_ref[2:3, :], 0.0)
    c = jnp.concatenate([head, c[8:tt - 8], tail], axis=0)
    hw = RWKV_WIDTH
    r = c[:, 0:hw]
    k = c[:, hw:2 * hw]
    v = c[:, 2 * hw:3 * hw]
    o = 3 * hw
    wd = c[:, o:o + W_LORA]
    ad = c[:, o + W_LORA:o + W_LORA + A_LORA]
    gd = c[:, o + W_LORA + A_LORA:o + W_LORA + A_LORA + G_LORA]
    ones = _head_ones()
    kscaled = k * kk_ref[...]
    norm = jnp.sqrt(_group_sum(kscaled * kscaled, ones))
    kkn = kscaled / jnp.maximum(norm, 1e-12)
    tw = jnp.tanh(wd)
    rk_sum = None
    for d in range(2):
        z = w0_ref[d:d + 1, :] + _dot3(tw, wup_ref[d])
        lw_out[d, 0] = -math.exp(-0.5) * _sigmoid(z)
        a = _sigmoid(a0_ref[d:d + 1, :] + _dot(ad, aup_ref[d]))
        a_out[d, 0] = a
        kd = k * (1.0 + (a - 1.0) * ka_ref[...])
        rk = r * kd * rk_ref[...]
        rk_sum = rk if d == 0 else rk_sum + rk
    r_out[0] = r
    k_out[0] = k
    v_out[0] = v
    kkn_out[0] = kkn
    g_out[0] = _dot(_sigmoid(gd), gup_ref[...])
    bonus_out[0] = _group_sum(rk_sum, ones) * v


def rwkv_prep(x, g_mix, w_in, layer, conv, w0, w_up, a0, a_up, g_up, k_k, k_a, r_k, tt):
    b, t, d = x.shape
    hw = RWKV_WIDTH
    nblk8 = t // 8
    full = lambda shape: pl.BlockSpec(shape, lambda bi, i: (0,) * len(shape))
    tok = pl.BlockSpec((1, tt, hw), lambda bi, i: (bi, i, 0))
    tok2 = pl.BlockSpec((2, 1, tt, hw), lambda bi, i: (0, bi, i, 0))
    sds = jax.ShapeDtypeStruct((b, t, hw), f32)
    sds2 = jax.ShapeDtypeStruct((2, b, t, hw), f32)
    return pl.pallas_call(
        functools.partial(_rwkv_prep_kernel, tt=tt),
        grid=(b, t // tt),
        in_specs=[pl.BlockSpec((1, tt, d), lambda bi, i: (bi, i, 0)),
                  pl.BlockSpec((1, 8, d), lambda bi, i: (bi, jnp.maximum(i * (tt // 8) - 1, 0), 0)),
                  pl.BlockSpec((1, 8, d), lambda bi, i: (bi, jnp.minimum((i + 1) * (tt // 8), nblk8 - 1), 0)),
                  full((1, d)),
                  pl.BlockSpec((None, d, RWKV_IN), lambda bi, i: (layer, 0, OFF_C // RWKV_IN)),
                  full((3, RWKV_IN)), full((2, hw)), full((2, W_LORA, hw)), full((2, hw)),
                  full((2, A_LORA, hw)), full((G_LORA, hw)), full((1, hw)), full((1, hw)), full((1, hw))],
        out_specs=[tok, tok, tok, tok, tok, tok, tok2, tok2],
        out_shape=[sds, sds, sds, sds, sds, sds, sds2, sds2],
        compiler_params=_params("parallel", "arbitrary"),
        name="rwkv_prep",
    )(x, x, x, g_mix.reshape(1, d), w_in, conv, w0, w_up, a0, a_up, g_up,
      k_k.reshape(1, hw), k_a.reshape(1, hw), r_k.reshape(1, hw))


def _head_masks():
    lane_head = lax.broadcasted_iota(jnp.int32, (CHUNK, WIDE), 1) // RWKV_HEAD
    return [jnp.where(lane_head == h, 1.0, 0.0).astype(bf16) for h in range(RWKV_HEADS)]


def _stack(x, masks):
    xb = x.astype(bf16)
    return jnp.concatenate([xb * mk for mk in masks], axis=0)


def _mm(a, b, masks):
    return _dot(a, _stack(b, masks))


def _unit_triangular_inverse(ms, eye, masks):
    r = lax.broadcasted_iota(jnp.int32, eye.shape, 0) // INV_BLOCK
    c = (lax.broadcasted_iota(jnp.int32, eye.shape, 1) % CHUNK) // INV_BLOCK
    mm = lambda a, b: _mm(a, b, masks)
    d = [jnp.where(r == c, m, 0.0) for m in ms]
    off = _each(lambda m, dd: m - dd, ms, d)
    x = [-dd for dd in d]
    x2 = _each(lambda xx: mm(xx, xx), x)
    x4 = _each(lambda xx: mm(xx, xx), x2)
    p1 = _each(lambda u, w: mm(eye + u, eye + w), x, x2)
    td = _each(lambda u, w: mm(u, eye + w), p1, x4)
    z = _each(mm, td, off)
    z2 = _each(lambda zz: mm(zz, zz), z)
    z4 = _each(lambda zz: mm(zz, zz), z2)
    p3 = _each(lambda u, w: mm(eye - u, eye + w), z, z2)
    p4 = _each(lambda w, t: mm(eye + w, t), z4, td)
    return _each(mm, p3, p4)


def _rwkv_chunks(revs, lws, a_s, ks, kkns, vs, rs, k_a, h_refs):
    masks = _head_masks()
    ci = lax.broadcasted_iota(jnp.int32, (CHUNK, CHUNK), 0)
    cj = lax.broadcasted_iota(jnp.int32, (CHUNK, CHUNK), 1)
    tri_f = jnp.where(cj <= ci, 1.0, 0.0).astype(bf16)
    tri_r = jnp.where(cj >= ci, 1.0, 0.0).astype(bf16)
    ti = lax.broadcasted_iota(jnp.int32, (CHUNK, WIDE), 0)
    tj = lax.broadcasted_iota(jnp.int32, (CHUNK, WIDE), 1) % CHUNK
    eye = jnp.where(ti == tj, 1.0, 0.0)
    strict = [(tj > ti) if rev else (tj < ti) for rev in revs]
    incl = [(tj >= ti) if rev else (tj <= ti) for rev in revs]

    g_incl = [_dot_exact_rhs(tri_r if rev else tri_f, lw) for rev, lw in zip(revs, lws)]
    g_tot = [g[0:1, :] if rev else g[CHUNK - 1:CHUNK, :] for rev, g in zip(revs, g_incl)]
    kat = _each(lambda kkn, a: kkn * a, kkns, a_s)
    kd = _each(lambda k, a: k * (1.0 + (a - 1.0) * k_a), ks, a_s)
    e_neg = [jnp.exp(-g) for g in g_incl]
    e_rem = _each(lambda gt, g: jnp.exp(gt - g), g_tot, g_incl)
    a_t = _each(lambda kkn, g, lw: kkn * jnp.exp(g - lw), kkns, g_incl, lws)
    r_t = _each(lambda r, g: r * jnp.exp(g), rs, g_incl)
    b_t = _each(lambda x, e: x * e, kat, e_neg)
    k_t = _each(lambda x, e: x * e, kd, e_neg)
    b_hat = _each(lambda x, e: x * e, kat, e_rem)
    k_hat = _each(lambda x, e: x * e, kd, e_rem)

    ar = _each(lambda x, y: jnp.concatenate([x, y], axis=0).astype(bf16), a_t, r_t)
    s_b = _each(lambda x, y: _dot_nt(x, _stack(y, masks)), ar, b_t)
    s_k = _each(lambda x, y: _dot_nt(x, _stack(y, masks)), ar, k_t)
    m_ab = _each(lambda s, x: jnp.where(s, x[:CHUNK], 0.0), strict, s_b)
    a_rb = _each(lambda s, x: jnp.where(s, x[CHUNK:], 0.0), incl, s_b)
    m_ak = _each(lambda s, x: jnp.where(s, x[:CHUNK], 0.0), strict, s_k)
    a_rk = _each(lambda s, x: jnp.where(s, x[CHUNK:], 0.0), incl, s_k)
    t_inv = _unit_triangular_inverse(m_ab, eye, masks)

    wi = lax.broadcasted_iota(jnp.int32, (WIDE, WIDE), 0)
    wj = lax.broadcasted_iota(jnp.int32, (WIDE, WIDE), 1)
    same_head = (wi // RWKV_HEAD) == (wj // RWKV_HEAD)
    diag = wi == wj
    h = [h_ref[...].astype(bf16) for h_ref in h_refs]
    decay = [jnp.where(diag, jnp.exp(gt), 0.0).astype(bf16) for gt in g_tot]
    arh = _each(lambda x, dd, hh: _dot(jnp.concatenate([x, dd], axis=0), hh), ar, decay, h)
    v_s = _each(lambda x: _stack(x, masks), vs)
    mkv = _each(lambda m, n, vv: _dot(jnp.concatenate([m, n], axis=0), vv), m_ak, a_rk, v_s)
    u = _each(lambda t, x, y: -_mm(t, x[:CHUNK] + y[:CHUNK], masks), t_inv, arh, mkv)
    ys = _each(lambda x, m, uu, y: x[CHUNK:2 * CHUNK] + _mm(m, uu, masks) + y[CHUNK:], arh, a_rb, u, mkv)
    bk_t = _each(lambda x, y: jnp.concatenate([x, y], axis=0).T, b_hat, k_hat)
    for h_ref, x, bk, uu, vv in zip(h_refs, arh, bk_t, u, vs):
        upd = _dot(bk, jnp.concatenate([uu, vv], axis=0))
        h_ref[...] = x[2 * CHUNK:] + jnp.where(same_head, upd, 0.0)
    return ys


def _rwkv_scan_kernel(ka_ref,
                      lw0, a0, k0, kk0, v0, r0,
                      lw1, a1, k1, kk1, v1, r1,
                      y0_ref, y1_ref, h_ref, *, rows):
    @pl.when(pl.program_id(1) == 0)
    def _():
        h_ref[...] = jnp.zeros_like(h_ref)

    revs = [False] * rows + [True] * rows
    gather = lambda f, b: [f[0, g] for g in range(rows)] + [b[0, g] for g in range(rows)]
    plain = lambda f, b: [f[g] for g in range(rows)] + [b[g] for g in range(rows)]
    ys = _rwkv_chunks(revs, gather(lw0, lw1), gather(a0, a1), plain(k0, k1), plain(kk0, kk1),
                      plain(v0, v1), plain(r0, r1), ka_ref[...], [h_ref.at[i] for i in range(2 * rows)])
    for g in range(rows):
        y0_ref[g] = ys[g]
        y1_ref[g] = ys[rows + g]


def rwkv_scan(lw, a, k, kkn, v, r, k_a, rows):
    b, t, hw = k.shape
    nc = t // CHUNK
    fwd = pl.BlockSpec((rows, CHUNK, hw), lambda bi, c: (bi, c, 0))
    bwd = pl.BlockSpec((rows, CHUNK, hw), lambda bi, c: (bi, nc - 1 - c, 0))
    fwd2 = pl.BlockSpec((1, rows, CHUNK, hw), lambda bi, c: (0, bi, c, 0))
    bwd2 = pl.BlockSpec((1, rows, CHUNK, hw), lambda bi, c: (1, bi, nc - 1 - c, 0))
    sds = jax.ShapeDtypeStruct((b, t, hw), f32)
    return pl.pallas_call(
        functools.partial(_rwkv_scan_kernel, rows=rows),
        grid=(b // rows, nc),
        in_specs=[pl.BlockSpec((1, hw), lambda bi, c: (0, 0)),
                  fwd2, fwd2, fwd, fwd, fwd, fwd,
                  bwd2, bwd2, bwd, bwd, bwd, bwd],
        out_specs=[fwd, bwd],
        out_shape=[sds, sds],
        scratch_shapes=[pltpu.VMEM((2 * rows, WIDE, WIDE), f32)],
        compiler_params=_params("parallel", "arbitrary"),
        name="rwkv_scan",
    )(k_a.reshape(1, hw), lw, a, k, kkn, v, r, lw, a, k, kkn, v, r)


def _rwkv_finish(y0, y1, g, bonus, ln_g, ln_b):
    ones = _head_ones()
    y = y0 + y1
    mu = _group_sum(y, ones) * (1.0 / RWKV_HEAD)
    yc = y - mu
    var = _group_sum(yc * yc, ones) * (1.0 / RWKV_HEAD)
    yn = yc * lax.rsqrt(var + GN_EPS) * ln_g + ln_b
    return (yn + bonus) * g


def _mix_cross_kernel(x_ref, a_ref, b_ref, y0_ref, y1_ref, g_ref, bonus_ref, lng_ref, lnb_ref,
                      wa_ref, wb_ref, wc_ref, gx_ref, wq_ref, kv_ref, wo_ref, o_ref):
    c = _rwkv_finish(y0_ref[0], y1_ref[0], g_ref[0], bonus_ref[0], lng_ref[...], lnb_ref[...])
    acc = jnp.dot(a_ref[0], wa_ref[...], preferred_element_type=f32)
    acc = acc + jnp.dot(b_ref[0], wb_ref[...], preferred_element_type=f32)
    acc = acc + jnp.dot(c.astype(bf16), wc_ref[...], preferred_element_type=f32)
    x = x_ref[0] + acc
    q = jnp.dot(_rms(x, gx_ref[...]).astype(bf16), wq_ref[...], preferred_element_type=f32)
    q = (q * (X_HEAD_DIM ** -0.5 * LOG2E)).astype(bf16)
    kv = kv_ref[0]
    xw = X_HEADS * X_HEAD_DIM
    heads = range(X_HEADS)
    s = [_dot_nt(q[:, h * X_HEAD_DIM:(h + 1) * X_HEAD_DIM], kv[:, h * X_HEAD_DIM:(h + 1) * X_HEAD_DIM])
         for h in heads]
    e = _each(lambda z: jnp.exp2(z - jnp.max(z, axis=-1, keepdims=True)), s)
    denom = _each(lambda z: jnp.sum(z, axis=-1, keepdims=True), e)
    pv = [jnp.dot(z.astype(bf16), kv[:, xw + h * X_HEAD_DIM:xw + (h + 1) * X_HEAD_DIM],
                  preferred_element_type=f32) for h, z in zip(heads, e)]
    o = jnp.concatenate(_each(lambda z, dd: z / dd, pv, denom), axis=-1)
    o_ref[0] = x + jnp.dot(o.astype(bf16), wo_ref[...], preferred_element_type=f32)


def mix_cross(x, o_win, o_diff, y0, y1, g, bonus, ln_g, ln_b, w_out, kv, g_cross, wq, wo, tm):
    b, t, d = x.shape
    nm = kv.shape[1]
    hw = RWKV_WIDTH
    xw = X_HEADS * X_HEAD_DIM
    wa = w_out[:WIN_WIDTH]
    wb = w_out[WIN_WIDTH:WIN_WIDTH + DIFF_WIDTH]
    wc = w_out[WIN_WIDTH + DIFF_WIDTH:]
    row = lambda w: pl.BlockSpec((1, tm, w), lambda bi, i: (bi, i, 0))
    full = lambda r, c: pl.BlockSpec((r, c), lambda bi, i: (0, 0))
    return pl.pallas_call(
        _mix_cross_kernel,
        grid=(b, t // tm),
        in_specs=[row(d), row(WIN_WIDTH), row(DIFF_WIDTH), row(hw), row(hw), row(hw), row(hw),
                  full(1, hw), full(1, hw), full(WIN_WIDTH, d), full(DIFF_WIDTH, d), full(hw, d),
                  full(1, d), full(d, xw),
                  pl.BlockSpec((1, nm, 2 * xw), lambda bi, i: (bi, 0, 0)),
                  full(xw, d)],
        out_specs=row(d),
        out_shape=jax.ShapeDtypeStruct((b, t, d), f32),
        compiler_params=_params("parallel", "parallel"),
        name="mix_cross",
    )(x, o_win, o_diff, y0, y1, g, bonus, ln_g.reshape(1, hw), ln_b.reshape(1, hw), wa, wb, wc,
      g_cross.reshape(1, d), wq, kv, wo)


def _ffn_up_kernel(x_ref, g_ref, wg_ref, wu_ref, o_ref):
    h = _rms(x_ref[...], g_ref[...]).astype(bf16)
    gate = jnp.dot(h, wg_ref[...], preferred_element_type=f32)
    up = jnp.dot(h, wu_ref[...], preferred_element_type=f32)
    o_ref[...] = (gate * _sigmoid(gate) * up).astype(o_ref.dtype)


def ffn_up(x, g, w_gu, layer, tm):
    m, d = x.shape
    return pl.pallas_call(
        _ffn_up_kernel,
        grid=(m // tm,),
        in_specs=[pl.BlockSpec((tm, d), lambda i: (i, 0)),
                  pl.BlockSpec((1, d), lambda i: (0, 0)),
                  pl.BlockSpec((None, d, D_FF), lambda i: (layer, 0, 0)),
                  pl.BlockSpec((None, d, D_FF), lambda i: (layer, 0, 1))],
        out_specs=pl.BlockSpec((tm, D_FF), lambda i: (i, 0)),
        out_shape=jax.ShapeDtypeStruct((m, D_FF), bf16),
        compiler_params=_params("parallel"),
        name="ffn_up",
    )(x, g.reshape(1, d), w_gu, w_gu)


def _ffn_down_kernel(x_ref, a_ref, w_ref, o_ref):
    o_ref[...] = x_ref[...] + jnp.dot(a_ref[...], w_ref[...], preferred_element_type=f32)


def _ffn_down_norm_kernel(x_ref, a_ref, w_ref, g_ref, o_ref):
    y = x_ref[...] + jnp.dot(a_ref[...], w_ref[...], preferred_element_type=f32)
    o_ref[...] = _rms(y, g_ref[...])


def ffn_down(x, act, w_down, layer, tm, final_gain=None):
    m, d = x.shape
    kf = act.shape[1]
    in_specs = [pl.BlockSpec((tm, d), lambda i: (i, 0)),
                pl.BlockSpec((tm, kf), lambda i: (i, 0)),
                pl.BlockSpec((None, kf, d), lambda i: (layer, 0, 0))]
    args = [x, act, w_down]
    if final_gain is not None:
        in_specs.append(pl.BlockSpec((1, d), lambda i: (0, 0)))
        args.append(final_gain.reshape(1, d))
    return pl.pallas_call(
        _ffn_down_kernel if final_gain is None else _ffn_down_norm_kernel,
        grid=(m // tm,),
        in_specs=in_specs,
        out_specs=pl.BlockSpec((tm, d), lambda i: (i, 0)),
        out_shape=jax.ShapeDtypeStruct((m, d), f32),
        compiler_params=_params("parallel"),
        name="ffn_down",
    )(*args)


def _row_tile(m, want):
    t = min(want, m)
    while m % t:
        t //= 2
    return t


class _Tiles(NamedTuple):
    rows: int
    ffn_rows: int
    seq: int
    mix: int
    mem_rows: int
    scan_rows: int


def _tile_plan(b, t, nm):
    m = b * t
    return _Tiles(rows=_row_tile(m, 1024), ffn_rows=_row_tile(m, 512), seq=_row_tile(t, 512),
                  mix=_row_tile(t, 1024), mem_rows=_row_tile(b * nm, 512), scan_rows=8 if b % 8 == 0 else 1)


def kernel(x, mem, norm_mix, w_in, win_sink, diff_lq1, diff_lk1, diff_lq2, diff_lk2, diff_subln,
           rwkv_conv, rwkv_w0, rwkv_w_up, rwkv_a0, rwkv_a_up, rwkv_g_up, rwkv_k_k, rwkv_k_a, rwkv_r_k,
           rwkv_ln_g, rwkv_ln_b, w_out, norm_cross, norm_mem, x_wq, x_wkv, x_wo, norm_ffn,
           ffn_w_gu, ffn_w_down, norm_final):
    b, t, d = x.shape
    nm = mem.shape[1]
    m = b * t
    tiles = _tile_plan(b, t, nm)
    depth = w_in.shape[0]
    w_in_b, w_out_b, wq_b, wkv_b, wo_b, wgu_b, wdn_b = (
        z.astype(bf16) for z in (w_in, w_out, x_wq, x_wkv, x_wo, ffn_w_gu, ffn_w_down))
    mem2 = mem.reshape(b * nm, d)
    xf = x.reshape(m, d)
    for l in range(depth):
        qa, kvb = (z.reshape(b, t, IN_GROUP) for z in in_proj(xf, norm_mix[l], w_in_b, l, tiles.rows))
        o_win = window_attention(qa, win_sink[l], tiles.seq)
        lam_init = 0.8 - 0.6 * math.exp(-0.3 * l)
        o_diff = diff_attention(qa, kvb, diff_lq1[l], diff_lk1[l], diff_lq2[l], diff_lk2[l], diff_subln[l],
                                lam_init, tiles.seq)
        r, k, v, kkn, g, bonus, lw, a = rwkv_prep(
            xf.reshape(b, t, d), norm_mix[l], w_in_b, l, rwkv_conv[l], rwkv_w0[l], rwkv_w_up[l], rwkv_a0[l],
            rwkv_a_up[l], rwkv_g_up[l], rwkv_k_k[l], rwkv_k_a[l], rwkv_r_k[l], tiles.seq)
        y0, y1 = rwkv_scan(lw, a, k, kkn, v, r, rwkv_k_a[l], tiles.scan_rows)
        kv = norm_matmul(mem2, norm_mem, wkv_b[l], tiles.mem_rows, 2 * X_HEADS * X_HEAD_DIM, bf16).reshape(b, nm, -1)
        xf = mix_cross(xf.reshape(b, t, d), o_win, o_diff, y0, y1, g, bonus, rwkv_ln_g[l], rwkv_ln_b[l],
                       w_out_b[l], kv, norm_cross[l], wq_b[l], wo_b[l], tiles.mix).reshape(m, d)
        act = ffn_up(xf, norm_ffn[l], wgu_b, l, tiles.ffn_rows)
        xf = ffn_down(xf, act, wdn_b, l, tiles.rows, norm_final if l == depth - 1 else None)
    return xf.reshape(b, t, d)
```
